```python
import jax
import jax.numpy as jnp
from jax import lax
import numpy as np

D_MODEL = 4096
BATCH = 2
SEQ = 8192
DEPTH = 2
DEC_BATCH = 1
DEC_SEQ = 8192
PAST_LEN = 128

HEAD_DIM = 128
N_HGRN_HEADS = 16
HGRN_WIDTH = N_HGRN_HEADS * HEAD_DIM
N_ATTN_HEADS = 16
N_KV_HEADS = 4
ATTN_WIDTH = N_ATTN_HEADS * HEAD_DIM
KV_WIDTH = N_KV_HEADS * HEAD_DIM
MIX_WIDTH = HGRN_WIDTH + ATTN_WIDTH
IN_COLS = 5 * HGRN_WIDTH + ATTN_WIDTH + 2 * KV_WIDTH
SPLIT_POINTS = (HGRN_WIDTH, 2 * HGRN_WIDTH, 3 * HGRN_WIDTH, 4 * HGRN_WIDTH, 5 * HGRN_WIDTH,
                5 * HGRN_WIDTH + ATTN_WIDTH, 5 * HGRN_WIDTH + ATTN_WIDTH + KV_WIDTH)
WINDOW = 128
ATTN_BLOCK = 128
ROPE_THETA = 10000.0
HGRN_CHUNK = 16
N_KEYS = 128
N_EXPERTS = N_KEYS * N_KEYS
PEER_HEADS = 8
PEER_TOPK = 16
PEER_HALF = 128
PEER_TOKEN_BLOCK = 128
EPS = 1e-6
NEG_INF = -1e30

kernel_name = "hymba_hgrn2_swa_peer_encoder"


def rmsnorm(x, g):
    xf = x.astype(jnp.float32)
    y = xf * lax.rsqrt(jnp.mean(xf * xf, axis=-1, keepdims=True) + EPS) * g.astype(jnp.float32)
    return y.astype(x.dtype)


def head_rmsnorm(o, g):
    H, dh = o.shape[-2], o.shape[-1]
    return o * lax.rsqrt(jnp.mean(o * o, axis=-1, keepdims=True) + EPS) * g.astype(jnp.float32).reshape(H, dh)


def rope(x):
    L, dh = x.shape[1], x.shape[-1]
    inv_freq = ROPE_THETA ** (-jnp.arange(0, dh, 2, dtype=jnp.float32) / dh)
    ang = jnp.arange(L, dtype=jnp.float32)[:, None] * inv_freq[None, :]
    ang = jnp.concatenate([ang, ang], axis=-1)[None, :, None, :]
    x1, x2 = jnp.split(x, 2, axis=-1)
    rot = jnp.concatenate([-x2, x1], axis=-1)
    return x * jnp.cos(ang) + rot * jnp.sin(ang)


def gated_linear_recurrence(q, k, v, log_f):
    B, H, L, dk = q.shape
    dv = v.shape[-1]
    C = HGRN_CHUNK
    n = L // C
    q = q.reshape(B, H, n, C, dk)
    k = k.reshape(B, H, n, C, dk)
    v = v.reshape(B, H, n, C, dv)
    b = jnp.cumsum(log_f.reshape(B, H, n, C, dk), axis=3)
    b_ref = b[:, :, :, C // 2 - 1:C // 2, :]
    b_last = b[:, :, :, C - 1:, :]
    causal = jnp.tril(jnp.ones((C, C), dtype=bool))
    scores = jnp.einsum('bhnik,bhnjk->bhnij', q * jnp.exp(b - b_ref), k * jnp.exp(b_ref - b))
    scores = jnp.where(causal, scores, 0.0)
    o_intra = jnp.einsum('bhnij,bhnjv->bhniv', scores, v)
    q_dec = jnp.moveaxis(q * jnp.exp(b), 2, 0)
    k_dec = jnp.moveaxis(k * jnp.exp(b_last - b), 2, 0)
    v_c = jnp.moveaxis(v, 2, 0)
    decay = jnp.moveaxis(jnp.exp(b_last[:, :, :, 0, :]), 2, 0)

    def step(S, xs):
        qd, kd, vc, dc = xs
        o = jnp.einsum('bhik,bhkv->bhiv', qd, S)
        S = dc[..., None] * S + jnp.einsum('bhjk,bhjv->bhkv', kd, vc)
        return S, o

    S0 = jnp.zeros((B, H, dk, dv), jnp.float32)
    _, o_inter = lax.scan(step, S0, (q_dec, k_dec, v_c, decay))
    o = o_intra + jnp.moveaxis(o_inter, 0, 2)
    return o.reshape(B, H, L, dv)


def hgrn2_group(q_z, ff_z, fb_z, i_z, g_z, lb, norm_g):
    B, L, _ = q_z.shape

    def heads(t):
        return t.reshape(B, L, N_HGRN_HEADS, HEAD_DIM).transpose(0, 2, 1, 3)

    def gate(z, lb_d):
        f = lb_d + (1.0 - lb_d) * jax.nn.sigmoid(z.astype(jnp.float32))
        return heads(jnp.log(f)), heads(1.0 - f)

    q = heads(jax.nn.silu(q_z.astype(jnp.float32)))
    v = heads(i_z.astype(jnp.float32))
    log_ff, k_f = gate(ff_z, lb[0])
    log_fb, k_b = gate(fb_z, lb[1])
    flip = lambda t: jnp.flip(t, axis=2)
    o_f = gated_linear_recurrence(q, k_f, v, log_ff)
    o_b = flip(gated_linear_recurrence(flip(q), flip(k_b), flip(v), flip(log_fb)))
    o = (o_f + o_b).transpose(0, 2, 1, 3)
    o = head_rmsnorm(o, norm_g).reshape(B, L, HGRN_WIDTH)
    return o * jax.nn.silu(g_z.astype(jnp.float32))


def window_attention(q, k, v, sink):
    B, L, H, dh = q.shape
    Hkv = k.shape[2]
    R = H // Hkv
    T = ATTN_BLOCK
    nb = L // T
    qb = q.reshape(B, nb, T, Hkv, R, dh)
    pad = ((0, 0), (T, T), (0, 0), (0, 0))
    kp = jnp.pad(k, pad).reshape(B, nb + 2, T, Hkv, dh)
    vp = jnp.pad(v, pad).reshape(B, nb + 2, T, Hkv, dh)
    kb = jnp.concatenate([kp[:, :-2], kp[:, 1:-1], kp[:, 2:]], axis=2)
    vb = jnp.concatenate([vp[:, :-2], vp[:, 1:-1], vp[:, 2:]], axis=2)
    s = jnp.einsum('bntgrd,bnsgd->bngrts', qb, kb) * (dh ** -0.5)
    blk = jnp.arange(nb)[:, None, None]
    qpos = blk * T + jnp.arange(T)[None, :, None]
    kpos = blk * T + jnp.arange(3 * T)[None, None, :] - T
    mask = (jnp.abs(qpos - kpos) <= WINDOW) & (kpos >= 0) & (kpos < L)
    s = jnp.where(mask[None, :, None, None], s, NEG_INF)
    sink_b = jnp.broadcast_to(sink.astype(jnp.float32).reshape(Hkv, R)[None, None, :, :, None, None],
                              s.shape[:-1] + (1,))
    p = jax.nn.softmax(jnp.concatenate([s, sink_b], axis=-1), axis=-1)[..., :-1]
    o = jnp.einsum('bngrts,bnsgd->bntgrd', p, vb)
    return o.reshape(B, L, H, dh)


def mixer(h, w_in_l, w_out_l, lb_l, hgrn_norm_l, attn_norm_l, sink_l):
    B, L, _ = h.shape
    p = h @ w_in_l
    q_h, ff_z, fb_z, i_z, g_z, q_a, k_a, v_a = jnp.split(p, list(SPLIT_POINTS), axis=-1)
    o_h = hgrn2_group(q_h, ff_z, fb_z, i_z, g_z, lb_l, hgrn_norm_l)
    qa = rope(q_a.astype(jnp.float32).reshape(B, L, N_ATTN_HEADS, HEAD_DIM))
    ka = rope(k_a.astype(jnp.float32).reshape(B, L, N_KV_HEADS, HEAD_DIM))
    va = v_a.astype(jnp.float32).reshape(B, L, N_KV_HEADS, HEAD_DIM)
    o_a = window_attention(qa, ka, va, sink_l)
    o_a = head_rmsnorm(o_a, attn_norm_l).reshape(B, L, ATTN_WIDTH)
    o = jnp.concatenate([o_h, o_a], axis=-1).astype(h.dtype)
    return o @ w_out_l


def peer(x, w_query, sub_keys, u_table, v_table):
    B, L, D = x.shape
    Tn = B * L
    xt = x.reshape(Tn, D)
    qy = (xt @ w_query).astype(jnp.float32).reshape(Tn, PEER_HEADS, 2, PEER_HALF)
    s = jnp.einsum('thcd,hckd->thck', qy, sub_keys.astype(jnp.float32))
    sv, si = lax.top_k(s, PEER_TOPK)
    cand = sv[:, :, 0, :, None] + sv[:, :, 1, None, :]
    cid = si[:, :, 0, :, None] * N_KEYS + si[:, :, 1, None, :]
    top_v, top_pos = lax.top_k(cand.reshape(Tn, PEER_HEADS, PEER_TOPK * PEER_TOPK), PEER_TOPK)
    expert = jnp.take_along_axis(cid.reshape(Tn, PEER_HEADS, PEER_TOPK * PEER_TOPK), top_pos, axis=-1)
    gate = jax.nn.softmax(top_v, axis=-1)
    E = PEER_HEADS * PEER_TOPK
    nblk = Tn // PEER_TOKEN_BLOCK

    def block(args):
        xb, eb, gb = args
        u = jnp.take(u_table, eb, axis=0)
        hid = jnp.einsum('ted,td->te', u, xb).astype(jnp.float32)
        w = (gb * jax.nn.gelu(hid, approximate=False)).astype(v_table.dtype)
        vv = jnp.take(v_table, eb, axis=0)
        return jnp.einsum('te,ted->td', w, vv)

    out = lax.map(block, (xt.reshape(nblk, PEER_TOKEN_BLOCK, D),
                          expert.reshape(nblk, PEER_TOKEN_BLOCK, E),
                          gate.reshape(nblk, PEER_TOKEN_BLOCK, E)))
    return out.reshape(B, L, D).astype(x.dtype)


def trunk(x, w_in, w_out, norm_mix, norm_ffn, lb, hgrn_norm, attn_norm, attn_sink,
          peer_query, peer_sub_keys, peer_u, peer_v, final_norm):
    for l in range(DEPTH):
        x = x + mixer(rmsnorm(x, norm_mix[l]), w_in[l], w_out[l], lb[l], hgrn_norm[l], attn_norm[l], attn_sink[l])
        x = x + peer(rmsnorm(x, norm_ffn[l]), peer_query[l], peer_sub_keys[l], peer_u[l], peer_v[l])
    return rmsnorm(x, final_norm)


def setup_inputs(seed: int = 0) -> dict:
    key = jax.random.key(seed)
    ks = jax.random.split(key, 16)
    f32 = jnp.float32
    nrm = lambda k, shape: jax.random.normal(k, shape, f32)
    return {
        "x_prompt": nrm(ks[0], (BATCH, SEQ, D_MODEL)),
        "x_sample": nrm(ks[1], (DEC_BATCH, DEC_SEQ, D_MODEL)),
        "w_in": nrm(ks[2], (DEPTH, D_MODEL, IN_COLS)) * D_MODEL ** -0.5,
        "w_out": nrm(ks[3], (DEPTH, MIX_WIDTH, D_MODEL)) * MIX_WIDTH ** -0.5,
        "norm_mix": 1.0 + 0.02 * nrm(ks[4], (DEPTH, D_MODEL)),
        "norm_ffn": 1.0 + 0.02 * nrm(ks[5], (DEPTH, D_MODEL)),
        "lb_logits": 0.5 * nrm(ks[6], (DEPTH, 2, HGRN_WIDTH)),
        "hgrn_norm": 1.0 + 0.02 * nrm(ks[7], (DEPTH, HGRN_WIDTH)),
        "attn_norm": 1.0 + 0.02 * nrm(ks[8], (DEPTH, ATTN_WIDTH)),
        "attn_sink": 0.5 * nrm(ks[9], (DEPTH, N_ATTN_HEADS)),
        "peer_query": nrm(ks[10], (DEPTH, D_MODEL, PEER_HEADS * 2 * PEER_HALF)) * D_MODEL ** -0.5,
        "peer_sub_keys": nrm(ks[11], (DEPTH, PEER_HEADS, 2, N_KEYS, PEER_HALF)) * PEER_HALF ** -0.5,
        "peer_u": nrm(ks[12], (DEPTH, N_EXPERTS, D_MODEL)) * D_MODEL ** -0.5,
        "peer_v": nrm(ks[13], (DEPTH, N_EXPERTS, D_MODEL)) * PEER_HEADS ** -0.5,
        "final_norm": 1.0 + 0.02 * nrm(ks[14], (D_MODEL,)),
    }


def reference(x_prompt, x_sample, w_in, w_out, norm_mix, norm_ffn, lb_logits, hgrn_norm, attn_norm,
              attn_sink, peer_query, peer_sub_keys, peer_u, peer_v, final_norm):
    lb = jnp.cumsum(jax.nn.softmax(lb_logits.astype(jnp.float32), axis=0), axis=0)
    lb = lb - lb[0:1]
    y_prompt = trunk(x_prompt, w_in, w_out, norm_mix, norm_ffn, lb, hgrn_norm, attn_norm, attn_sink,
                     peer_query, peer_sub_keys, peer_u, peer_v, final_norm)
    y_sample = trunk(x_sample, w_in, w_out, norm_mix, norm_ffn, lb, hgrn_norm, attn_norm, attn_sink,
                     peer_query, peer_sub_keys, peer_u, peer_v, final_norm)
    return (y_prompt, y_sample)
```

```python
import functools

import jax
import jax.numpy as jnp
from jax import lax
from jax.experimental import pallas as pl
from jax.experimental.pallas import tpu as pltpu

F32 = jnp.float32
BF16 = jnp.bfloat16

HEAD_DIM = 128
N_HGRN_HEADS = 16
N_ATTN_HEADS = 16
N_KV_HEADS = 4
GQA_REP = N_ATTN_HEADS // N_KV_HEADS
WINDOW = 128
ROPE_THETA = 10000.0
HGRN_SUB = 16
N_KEYS = 128
PEER_HEADS = 8
PEER_TOPK = 16
EPS = 1e-6
NEG_INF = -1e30

V7X_VMEM_BYTES = 64 * 1024 * 1024
VMEM_LIMIT = 56 * 1024 * 1024


def _params(sem, est_bytes=None):
    return pltpu.CompilerParams(dimension_semantics=sem, vmem_limit_bytes=VMEM_LIMIT)


def _sigmoid(x):
    return 1.0 / (1.0 + jnp.exp(-x))


def _dot_nt(a, b):
    return lax.dot_general(a, b, (((1,), (1,)), ((), ())), preferred_element_type=F32)


def _norm_kernel(x_ref, g_ref, h_ref):
    x = x_ref[...]
    y = x * lax.rsqrt(jnp.mean(x * x, axis=-1, keepdims=True) + EPS) * g_ref[...]
    h_ref[...] = y.astype(h_ref.dtype)


def _addnorm_kernel(x_ref, y_ref, g_ref, s_ref, h_ref):
    x = x_ref[...] + y_ref[...]
    s_ref[...] = x
    y = x * lax.rsqrt(jnp.mean(x * x, axis=-1, keepdims=True) + EPS) * g_ref[...]
    h_ref[...] = y.astype(h_ref.dtype)


def _addnorm_final_kernel(x_ref, y_ref, g_ref, h_ref):
    x = x_ref[...] + y_ref[...]
    y = x * lax.rsqrt(jnp.mean(x * x, axis=-1, keepdims=True) + EPS) * g_ref[...]
    h_ref[...] = y.astype(h_ref.dtype)


def _row_tile(t, want):
    return want if t % want == 0 else t


def rmsnorm_rows(x, g, out_dtype):
    t, d = x.shape
    tm = _row_tile(t, 256)
    row = pl.BlockSpec((tm, d), lambda i: (i, 0))
    return pl.pallas_call(
        _norm_kernel,
        out_shape=jax.ShapeDtypeStruct((t, d), out_dtype),
        grid=(t // tm,),
        in_specs=[row, pl.BlockSpec((1, d), lambda i: (0, 0))],
        out_specs=row,
        compiler_params=_params(("parallel",)),
        name="rmsnorm",
    )(x, g.reshape(1, d))


def add_rmsnorm_rows(x, y, g, out_dtype, keep_sum):
    t, d = x.shape
    tm = _row_tile(t, 256)
    row = pl.BlockSpec((tm, d), lambda i: (i, 0))
    gspec = pl.BlockSpec((1, d), lambda i: (0, 0))
    if keep_sum:
        return pl.pallas_call(
            _addnorm_kernel,
            out_shape=(jax.ShapeDtypeStruct((t, d), F32), jax.ShapeDtypeStruct((t, d), out_dtype)),
            grid=(t // tm,),
            in_specs=[row, row, gspec],
            out_specs=(row, row),
            compiler_params=_params(("parallel",)),
            name="add_rmsnorm",
        )(x, y, g.reshape(1, d))
    return pl.pallas_call(
        _addnorm_final_kernel,
        out_shape=jax.ShapeDtypeStruct((t, d), out_dtype),
        grid=(t // tm,),
        in_specs=[row, row, gspec],
        out_specs=row,
        compiler_params=_params(("parallel",)),
        name="add_rmsnorm_final",
    )(x, y, g.reshape(1, d))


def _mm_kernel(a_ref, b_ref, o_ref):
    o_ref[...] = jnp.dot(a_ref[...], b_ref[...], preferred_element_type=F32)


def _mm_slab_kernel(a_ref, b_ref, o_ref):
    acc = jnp.dot(a_ref[...], b_ref[...], preferred_element_type=F32)
    for j in range(o_ref.shape[0]):
        o_ref[j] = acc[:, j * HEAD_DIM:(j + 1) * HEAD_DIM]


def matmul(a, b, tn, slab_out=False):
    t, k = a.shape
    n = b.shape[1]
    tm = _row_tile(t, 1024)
    grid = (t // tm, n // tn)
    in_specs = [pl.BlockSpec((tm, k), lambda i, j: (i, 0)), pl.BlockSpec((k, tn), lambda i, j: (0, j))]
    if slab_out:
        ns = tn // HEAD_DIM
        return pl.pallas_call(
            _mm_slab_kernel,
            out_shape=jax.ShapeDtypeStruct((n // HEAD_DIM, t, HEAD_DIM), F32),
            grid=grid,
            in_specs=in_specs,
            out_specs=pl.BlockSpec((ns, tm, HEAD_DIM), lambda i, j: (j, i, 0)),
            compiler_params=_params(("parallel", "parallel")),
            name="matmul_slab",
        )(a, b)
    return pl.pallas_call(
        _mm_kernel,
        out_shape=jax.ShapeDtypeStruct((t, n), F32),
        grid=grid,
        in_specs=in_specs,
        out_specs=pl.BlockSpec((tm, tn), lambda i, j: (i, j)),
        compiler_params=_params(("parallel", "parallel")),
        name="matmul",
    )(a, b)


def _mm2_res_kernel(a1_ref, a2_ref, b1_ref, b2_ref, r_ref, o_ref):
    acc = jnp.dot(a1_ref[...], b1_ref[...], preferred_element_type=F32)
    acc = acc + jnp.dot(a2_ref[...], b2_ref[...], preferred_element_type=F32)
    o_ref[...] = acc + r_ref[...]


def matmul2_residual(a1, a2, b, res, tn):
    t, kh = a1.shape
    n = b.shape[1]
    tm = _row_tile(t, 1024)
    return pl.pallas_call(
        _mm2_res_kernel,
        out_shape=jax.ShapeDtypeStruct((t, n), F32),
        grid=(t // tm, n // tn),
        in_specs=[
            pl.BlockSpec((tm, kh), lambda i, j: (i, 0)),
            pl.BlockSpec((tm, kh), lambda i, j: (i, 0)),
            pl.BlockSpec((kh, tn), lambda i, j: (0, j)),
            pl.BlockSpec((kh, tn), lambda i, j: (1, j)),
            pl.BlockSpec((tm, tn), lambda i, j: (i, j)),
        ],
        out_specs=pl.BlockSpec((tm, tn), lambda i, j: (i, j)),
        compiler_params=_params(("parallel", "parallel")),
        name="matmul2_residual",
    )(a1, a2, b, b, res)


HGRN_BLOCK = 128


def _hgrn_masks(cb, reverse):
    ti = lax.broadcasted_iota(jnp.int32, (cb, cb), 0)
    ji = lax.broadcasted_iota(jnp.int32, (cb, cb), 1)
    same = (ti // HGRN_SUB) == (ji // HGRN_SUB)
    masks = [jnp.logical_and(same, (ji >= ti) if reverse else (ji <= ti))]
    hs = HGRN_SUB
    while 2 * hs <= cb:
        grp = (ti // (2 * hs)) == (ji // (2 * hs))
        t_hi = (ti % (2 * hs)) >= hs
        j_hi = (ji % (2 * hs)) >= hs
        if reverse:
            sel = jnp.logical_and(jnp.logical_not(t_hi), j_hi)
        else:
            sel = jnp.logical_and(t_hi, jnp.logical_not(j_hi))
        masks.append(jnp.logical_and(grp, sel))
        hs *= 2
    tri = ((ji >= ti) if reverse else (ji <= ti)).astype(BF16)
    return masks, tri


def _ref_rows(b, group, idx):
    cb = b.shape[0]
    g = b.reshape(cb // group, group, HEAD_DIM)
    r = jnp.broadcast_to(g[:, idx:idx + 1, :], g.shape)
    return r.reshape(cb, HEAD_DIM)


def _cumsum_rows(tri, a):
    a1 = a.astype(BF16)
    r1 = a - a1.astype(F32)
    a2 = r1.astype(BF16)
    a3 = (r1 - a2.astype(F32)).astype(BF16)
    out = jnp.dot(tri, jnp.concatenate([a1, a2, a3], axis=1), preferred_element_type=F32)
    return out[:, :HEAD_DIM] + out[:, HEAD_DIM:2 * HEAD_DIM] + out[:, 2 * HEAD_DIM:]


def _hgrn_head(qz, fz, v, lb, st, masks, tri, reverse):
    cb = qz.shape[0]
    q = qz * _sigmoid(qz)
    f = lb + (1.0 - lb) * _sigmoid(fz)
    a = jnp.log(f)
    k = 1.0 - f
    b = _cumsum_rows(tri, a)
    b_tot = b[0:1, :] if reverse else b[cb - 1:cb, :]

    def scores(eq, ek, mask):
        s = _dot_nt((q * eq).astype(BF16), (k * ek).astype(BF16))
        return jnp.where(mask, s, 0.0)

    r0 = _ref_rows(b, HGRN_SUB, HGRN_SUB // 2 if reverse else HGRN_SUB // 2 - 1)
    att = scores(jnp.exp(b - r0), jnp.exp(r0 - b), masks[0])
    hs = HGRN_SUB
    for mask in masks[1:]:
        r = _ref_rows(b, 2 * hs, hs if reverse else hs - 1)
        att = att + scores(jnp.exp(jnp.minimum(b - r, 0.0)), jnp.exp(jnp.minimum(r - b, 0.0)), mask)
        hs *= 2
    vb = v.astype(BF16)
    o = jnp.dot(att.astype(BF16), vb, preferred_element_type=F32)
    o = o + _dot_nt((q * jnp.exp(b)).astype(BF16), st.astype(BF16))
    k_dec = (k * jnp.exp(b_tot - b)).astype(BF16)
    st_new = st * jnp.exp(b_tot) + jnp.dot(v.T.astype(BF16), k_dec, preferred_element_type=F32)
    return o, st_new


def _lower_bound(lbl, layer):
    if layer == 0:
        return jnp.zeros((1, lbl.shape[1]), F32)
    m = jnp.max(lbl, axis=0, keepdims=True)
    e = jnp.exp(lbl - m)
    sm = e / jnp.sum(e, axis=0, keepdims=True)
    return jnp.sum(sm[1:layer + 1, :], axis=0, keepdims=True)


def _hgrn_fwd_kernel(layer, q_ref, f_ref, i_ref, lbl_ref, o_ref, st_ref):
    @pl.when(pl.program_id(1) == 0)
    def _():
        st_ref[...] = jnp.zeros_like(st_ref)

    cb = q_ref.shape[0]
    masks, tri = _hgrn_masks(cb, False)
    lb = _lower_bound(lbl_ref[...], layer)
    for h in range(N_HGRN_HEADS):
        sl = slice(h * HEAD_DIM, (h + 1) * HEAD_DIM)
        o, st = _hgrn_head(q_ref[:, sl], f_ref[:, sl], i_ref[:, sl], lb[:, sl], st_ref[h], masks, tri, False)
        st_ref[h] = st
        o_ref[:, sl] = o


def _hgrn_bwd_kernel(layer, q_ref, f_ref, i_ref, g_ref, of_ref, lbl_ref, nw_ref, o_ref, st_ref):
    @pl.when(pl.program_id(1) == 0)
    def _():
        st_ref[...] = jnp.zeros_like(st_ref)

    cb = q_ref.shape[0]
    masks, tri = _hgrn_masks(cb, True)
    lb = _lower_bound(lbl_ref[...], layer)
    for h in range(N_HGRN_HEADS):
        sl = slice(h * HEAD_DIM, (h + 1) * HEAD_DIM)
        o, st = _hgrn_head(q_ref[:, sl], f_ref[:, sl], i_ref[:, sl], lb[:, sl], st_ref[h], masks, tri, True)
        st_ref[h] = st
        o = o + of_ref[:, sl]
        o = o * lax.rsqrt(jnp.mean(o * o, axis=-1, keepdims=True) + EPS) * nw_ref[:, sl]
        gz = g_ref[:, sl]
        o_ref[:, sl] = (o * (gz * _sigmoid(gz))).astype(o_ref.dtype)


def hgrn2(p, lb_logits_t, norm_w, layer, n_seq, seq_len):
    t = p.shape[0]
    w = N_HGRN_HEADS * HEAD_DIM
    cb = HGRN_BLOCK
    nb = seq_len // cb
    depth = lb_logits_t.shape[1]

    def col(j, reverse):
        if reverse:
            return pl.BlockSpec((cb, w), lambda s, c: (s * nb + nb - 1 - c, j))
        return pl.BlockSpec((cb, w), lambda s, c: (s * nb + c, j))

    def lbl(d):
        return pl.BlockSpec((None, depth, w), lambda s, c: (d, 0, 0))

    scratch = [pltpu.VMEM((N_HGRN_HEADS, HEAD_DIM, HEAD_DIM), F32)]
    o_f = pl.pallas_call(
        functools.partial(_hgrn_fwd_kernel, layer),
        out_shape=jax.ShapeDtypeStruct((t, w), F32),
        grid=(n_seq, nb),
        in_specs=[col(0, False), col(1, False), col(3, False), lbl(0)],
        out_specs=col(0, False),
        scratch_shapes=scratch,
        compiler_params=_params(("parallel", "arbitrary")),
        name="hgrn_fwd",
    )(p, p, p, lb_logits_t)
    return pl.pallas_call(
        functools.partial(_hgrn_bwd_kernel, layer),
        out_shape=jax.ShapeDtypeStruct((t, w), BF16),
        grid=(n_seq, nb),
        in_specs=[col(0, True), col(2, True), col(3, True), col(4, True), col(0, True), lbl(1),
                  pl.BlockSpec((1, w), lambda s, c: (0, 0))],
        out_specs=col(0, True),
        scratch_shapes=scratch,
        compiler_params=_params(("parallel", "arbitrary")),
        name="hgrn_bwd",
    )(p, p, p, p, o_f, lb_logits_t, norm_w.reshape(1, w))


ATTN_BLOCK = WINDOW


def _rope(x, cos, sin_signed):
    return x * cos + pltpu.roll(x, HEAD_DIM // 2, axis=1) * sin_signed


def _attn_kernel(q_ref, kp_ref, kc_ref, kn_ref, vp_ref, vc_ref, vn_ref,
                 cp_ref, sp_ref, cc_ref, sc_ref, cn_ref, sn_ref, sink_ref, nw_ref, o_ref):
    n = pl.program_id(1)
    nb = pl.num_programs(1)
    tb = q_ref.shape[0]
    ri = lax.broadcasted_iota(jnp.int32, (tb, tb), 0)
    ci = lax.broadcasted_iota(jnp.int32, (tb, tb), 1)
    mask = jnp.concatenate([
        jnp.logical_and(ci >= ri, n > 0),
        jnp.full((tb, tb), True),
        jnp.logical_and(ci <= ri, n < nb - 1),
    ], axis=1)
    scale = HEAD_DIM ** -0.5
    cc, sc = cc_ref[...], sc_ref[...]
    for g in range(N_KV_HEADS):
        sl = slice(g * HEAD_DIM, (g + 1) * HEAD_DIM)
        kcat = jnp.concatenate([
            _rope(kp_ref[:, sl], cp_ref[...], sp_ref[...]),
            _rope(kc_ref[:, sl], cc, sc),
            _rope(kn_ref[:, sl], cn_ref[...], sn_ref[...]),
        ], axis=0).astype(BF16)
        vcat = jnp.concatenate([vp_ref[:, sl], vc_ref[:, sl], vn_ref[:, sl]], axis=0).astype(BF16)
        for r in range(GQA_REP):
            h = g * GQA_REP + r
            hs = slice(h * HEAD_DIM, (h + 1) * HEAD_DIM)
            q = _rope(q_ref[:, hs], cc, sc).astype(BF16)
            s = _dot_nt(q, kcat) * scale
            s = jnp.where(mask, s, NEG_INF)
            sink = sink_ref[h]
            m = jnp.maximum(jnp.max(s, axis=-1, keepdims=True), sink)
            e = jnp.exp(s - m)
            den = jnp.sum(e, axis=-1, keepdims=True) + jnp.exp(sink - m)
            o = jnp.dot(e.astype(BF16), vcat, preferred_element_type=F32) / den
            o = o * lax.rsqrt(jnp.mean(o * o, axis=-1, keepdims=True) + EPS) * nw_ref[:, hs]
            o_ref[:, hs] = o.astype(o_ref.dtype)


def window_attention(p, cos, sin_signed, sink, norm_w, n_seq, seq_len, q_col, k_col, v_col):
    t = p.shape[0]
    tb = ATTN_BLOCK
    nb = seq_len // tb
    qw = N_ATTN_HEADS * HEAD_DIM
    kw = N_KV_HEADS * HEAD_DIM

    def prev(n):
        return jnp.maximum(n - 1, 0)

    def nxt(n):
        return jnp.minimum(n + 1, nb - 1)

    def kv(colblk, f):
        return pl.BlockSpec((tb, kw), lambda s, n: (s * nb + f(n), colblk))

    def tab(f):
        return pl.BlockSpec((tb, HEAD_DIM), lambda s, n: (f(n), 0))

    same = lambda n: n
    qspec = pl.BlockSpec((tb, qw), lambda s, n: (s * nb + n, q_col // qw))
    kb, vb = k_col // kw, v_col // kw
    return pl.pallas_call(
        _attn_kernel,
        out_shape=jax.ShapeDtypeStruct((t, qw), BF16),
        grid=(n_seq, nb),
        in_specs=[qspec, kv(kb, prev), kv(kb, same), kv(kb, nxt), kv(vb, prev), kv(vb, same), kv(vb, nxt),
                  tab(prev), tab(prev), tab(same), tab(same), tab(nxt), tab(nxt),
                  pl.BlockSpec(memory_space=pltpu.SMEM),
                  pl.BlockSpec((1, qw), lambda s, n: (0, 0))],
        out_specs=pl.BlockSpec((tb, qw), lambda s, n: (s * nb + n, 0)),
        compiler_params=_params(("parallel", "parallel")),
        name="window_attention",
    )(p, p, p, p, p, p, p, cos, sin_signed, cos, sin_signed, cos, sin_signed, sink, norm_w.reshape(1, qw))


def rope_tables(seq_len):
    inv_freq = ROPE_THETA ** (-jnp.arange(0, HEAD_DIM, 2, dtype=F32) / HEAD_DIM)
    ang = jnp.arange(seq_len, dtype=F32)[:, None] * inv_freq[None, :]
    ang = jnp.concatenate([ang, ang], axis=-1)
    sign = jnp.concatenate([-jnp.ones((HEAD_DIM // 2,), F32), jnp.ones((HEAD_DIM // 2,), F32)])
    return jnp.cos(ang), jnp.sin(ang) * sign[None, :]


ROUTE_TILE = 256


def _extract_top(s, iota, k_out, v_scr, i_scr):
    n = s.shape[0]
    for k in range(k_out):
        m = jnp.max(s, axis=0, keepdims=True)
        idx = jnp.min(jnp.where(s == m, iota, float(n)), axis=0, keepdims=True)
        s = jnp.where(iota == idx, -jnp.inf, s)
        v_scr[k:k + 1, :] = m
        i_scr[k:k + 1, :] = idx


def _route_kernel(qy_ref, sk_ref, i_ref, j_ref, g_ref, sv0, si0, sv1, si1, tv, tp, ti, tj, tg):
    tm = qy_ref.shape[1]
    kio = lax.broadcasted_iota(jnp.int32, (N_KEYS, tm), 0).astype(F32)
    pio = lax.broadcasted_iota(jnp.int32, (PEER_TOPK * PEER_TOPK, tm), 0).astype(F32)

    def head(h, carry):
        for c, (vs, is_) in enumerate(((sv0, si0), (sv1, si1))):
            keys = sk_ref[h, c].astype(BF16)
            q = qy_ref[2 * h + c].astype(BF16)
            _extract_top(_dot_nt(keys, q), kio, PEER_TOPK, vs, is_)
        v1 = sv1[...]
        cand = jnp.concatenate([sv0[a:a + 1, :] + v1 for a in range(PEER_TOPK)], axis=0)
        _extract_top(cand, pio, PEER_TOPK, tv, tp)
        top_v, pos = tv[...], tp[...]
        a_sel = jnp.floor(pos * (1.0 / PEER_TOPK))
        b_sel = pos - a_sel * PEER_TOPK
        ki = jnp.zeros_like(pos)
        kj = jnp.zeros_like(pos)
        for a in range(PEER_TOPK):
            ki = ki + jnp.where(a_sel == float(a), si0[a:a + 1, :], 0.0)
            kj = kj + jnp.where(b_sel == float(a), si1[a:a + 1, :], 0.0)
        e = jnp.exp(top_v - top_v[0:1, :])
        gate = e / jnp.sum(e, axis=0, keepdims=True)
        row = pl.multiple_of(h * PEER_TOPK, PEER_TOPK)
        ti[pl.ds(row, PEER_TOPK), :] = ki
        tj[pl.ds(row, PEER_TOPK), :] = kj
        tg[pl.ds(row, PEER_TOPK), :] = gate
        return carry

    lax.fori_loop(0, PEER_HEADS, head, 0)
    i_ref[...] = ti[...].T
    j_ref[...] = tj[...].T
    g_ref[...] = tg[...].T


def peer_route(qy_slab, sub_keys):
    t = qy_slab.shape[1]
    tm = _row_tile(t, ROUTE_TILE)
    slots = PEER_HEADS * PEER_TOPK
    out = jax.ShapeDtypeStruct((t, slots), F32)
    ospec = pl.BlockSpec((tm, slots), lambda i: (i, 0))
    small = lambda: pltpu.VMEM((PEER_TOPK, tm), F32)
    big = lambda: pltpu.VMEM((slots, tm), F32)
    return pl.pallas_call(
        _route_kernel,
        out_shape=(out, out, out),
        grid=(t // tm,),
        in_specs=[pl.BlockSpec((2 * PEER_HEADS, tm, HEAD_DIM), lambda i: (0, i, 0)),
                  pl.BlockSpec(sub_keys.shape, lambda i: (0, 0, 0, 0))],
        out_specs=(ospec, ospec, ospec),
        scratch_shapes=[small(), small(), small(), small(), small(), small(), big(), big(), big()],
        compiler_params=_params(("parallel",)),
        name="peer_route",
    )(qy_slab, sub_keys)


COEF_TILE = 256


def _coef_kernel(i_ref, j_ref, g_ref, c_ref):
    tm = i_ref.shape[0]
    sub = lax.broadcasted_iota(jnp.int32, (N_KEYS, i_ref.shape[1]), 0).astype(F32)

    def tok(t, carry):
        irow = i_ref[pl.ds(t, 1), :]
        jrow = j_ref[pl.ds(t, 1), :]
        grow = g_ref[pl.ds(t, 1), :]
        ptg = jnp.where(sub == irow, grow, 0.0).astype(BF16)
        qt = jnp.where(sub == jrow, 1.0, 0.0).astype(BF16)
        c_ref[t] = _dot_nt(ptg, qt).astype(c_ref.dtype)
        return carry

    lax.fori_loop(0, tm, tok, 0, unroll=4)


def peer_coefficients(ki, kj, gate):
    t, slots = ki.shape
    tm = _row_tile(t, COEF_TILE)
    spec = pl.BlockSpec((tm, slots), lambda i: (i, 0))
    c = pl.pallas_call(
        _coef_kernel,
        out_shape=jax.ShapeDtypeStruct((t, N_KEYS, N_KEYS), BF16),
        grid=(t // tm,),
        in_specs=[spec, spec, spec],
        out_specs=pl.BlockSpec((tm, N_KEYS, N_KEYS), lambda i: (i, 0, 0)),
        compiler_params=_params(("parallel",)),
        name="peer_coefficients",
    )(ki, kj, gate)
    return c.reshape(t, N_KEYS * N_KEYS)


def _gelu(x):
    return 0.5 * x * (1.0 + lax.erf(x * 0.7071067811865476))


def _peer_kernel(x_ref, u_ref, v_ref, c_ref, o_ref):
    @pl.when(pl.program_id(1) == 0)
    def _():
        o_ref[...] = jnp.zeros_like(o_ref)

    hid = _dot_nt(x_ref[...], u_ref[...])
    w = (c_ref[...].astype(F32) * _gelu(hid)).astype(BF16)
    o_ref[...] += jnp.dot(w, v_ref[...], preferred_element_type=F32)


def peer_dense(x, u, v, coef, tm_want=512, te=512):
    t, d = x.shape
    e = u.shape[0]
    tm = _row_tile(t, tm_want)
    return pl.pallas_call(
        _peer_kernel,
        out_shape=jax.ShapeDtypeStruct((t, d), F32),
        grid=(t // tm, e // te),
        in_specs=[pl.BlockSpec((tm, d), lambda i, j: (i, 0)),
                  pl.BlockSpec((te, d), lambda i, j: (j, 0)),
                  pl.BlockSpec((te, d), lambda i, j: (j, 0)),
                  pl.BlockSpec((tm, te), lambda i, j: (i, j))],
        out_specs=pl.BlockSpec((tm, d), lambda i, j: (i, 0)),
        compiler_params=_params(("parallel", "arbitrary")),
        name="peer_dense",
    )(x, u, v, coef)


def kernel(x_prompt, x_sample, w_in, w_out, norm_mix, norm_ffn, lb_logits, hgrn_norm, attn_norm, attn_sink,
           peer_query, peer_sub_keys, peer_u, peer_v, final_norm):
    depth = w_in.shape[0]
    d_model = x_prompt.shape[-1]
    seq_len = x_prompt.shape[1]
    assert x_sample.shape[1] == seq_len, "prompt and sample sequences are stacked and must share a length"
    n_prompt, n_sample = x_prompt.shape[0], x_sample.shape[0]
    n_seq = n_prompt + n_sample
    hw = N_HGRN_HEADS * HEAD_DIM
    q_col = 5 * hw
    k_col = q_col + N_ATTN_HEADS * HEAD_DIM
    v_col = k_col + N_KV_HEADS * HEAD_DIM

    x = jnp.concatenate([x_prompt.reshape(-1, d_model), x_sample.reshape(-1, d_model)], axis=0)
    cos, sin_signed = rope_tables(seq_len)
    lbl_t = jnp.transpose(lb_logits.astype(F32), (1, 0, 2))

    h = rmsnorm_rows(x, norm_mix[0], BF16)
    for l in range(depth):
        p = matmul(h, w_in[l].astype(BF16), tn=512)
        o_h = hgrn2(p, lbl_t, hgrn_norm[l], l, n_seq, seq_len)
        o_a = window_attention(p, cos, sin_signed, attn_sink[l].astype(F32), attn_norm[l], n_seq, seq_len,
                               q_col, k_col, v_col)
        x = matmul2_residual(o_h, o_a, w_out[l].astype(BF16), x, tn=512)
        h2 = rmsnorm_rows(x, norm_ffn[l], BF16)
        qy = matmul(h2, peer_query[l].astype(BF16), tn=512, slab_out=True)
        ki, kj, gate = peer_route(qy, peer_sub_keys[l])
        coef = peer_coefficients(ki, kj, gate)
        po = peer_dense(h2, peer_u[l].astype(BF16), peer_v[l].astype(BF16), coef)
        if l + 1 < depth:
            x, h = add_rmsnorm_rows(x, po, norm_mix[l + 1], BF16, keep_sum=True)
        else:
            y = add_rmsnorm_rows(x, po, final_norm, F32, keep_sum=False)
    t_prompt = n_prompt * seq_len
    return (y[:t_prompt].reshape(x_prompt.shape), y[t_prompt:].reshape(x_sample.shape))
```

```python
import functools

import jax
import jax.numpy as jnp
from jax import lax
from jax.experimental import pallas as pl
from jax.experimental.pallas import tpu as pltpu

F32 = jnp.float32
BF16 = jnp.bfloat16

LANES = 128
SUBLANES = 8
HEAD_DIM = LANES
N_HGRN_HEADS = 16
N_ATTN_HEADS = 16
N_KV_HEADS = 4
GQA_REP = N_ATTN_HEADS // N_KV_HEADS
WINDOW = 128
ROPE_THETA = 10000.0
HGRN_SUB = 16
N_KEYS = 128
PEER_HEADS = 8
PEER_TOPK = 16
EPS = 1e-6
NEG_INF = -1e30
LOG2_E = 1.4426950408889634

VMEM_LIMIT = 56 * 1024 * 1024

NORM_ROWS = 256
MM_ROWS = 1024
MM_COLS = 512
CAST_BLOCK = 1024
HGRN_BLOCK = 128
HGRN_GROUP = 4
ATTN_BLOCK = WINDOW
ROUTE_TILE = 256
ROUTE_UNROLL = 4
COEF_TILE = 256
COEF_GROUP = 16
COEF_PITCH = N_KEYS + SUBLANES
PEER_ROWS = 512
PEER_EXPERTS = 512


def _params(sem):
    return pltpu.CompilerParams(dimension_semantics=sem, vmem_limit_bytes=VMEM_LIMIT)


def _sigmoid(x):
    return 1.0 / (1.0 + jnp.exp(-x))


def _dot_nt(a, b):
    return lax.dot_general(a, b, (((1,), (1,)), ((), ())), preferred_element_type=F32)


def _row_tile(t, want):
    return want if t % want == 0 else t


def _common_tile(want, *sizes):
    while any(n % want for n in sizes):
        want //= 2
    return want


def _rms(x, g):
    return x * lax.rsqrt(jnp.mean(x * x, axis=-1, keepdims=True) + EPS) * g


def _cast_kernel(w_ref, o_ref):
    o_ref[...] = w_ref[...].astype(o_ref.dtype)


def cast_bf16(w):
    r, c = w.shape
    br, bc = _row_tile(r, CAST_BLOCK), _row_tile(c, CAST_BLOCK)
    spec = pl.BlockSpec((br, bc), lambda i, j: (i, j))
    return pl.pallas_call(
        _cast_kernel,
        out_shape=jax.ShapeDtypeStruct((r, c), BF16),
        grid=(r // br, c // bc),
        in_specs=[spec],
        out_specs=spec,
        compiler_params=_params(("parallel", "parallel")),
        name="cast_bf16",
    )(w)


def _norm2_kernel(n_a, xa_ref, xb_ref, g_ref, h_ref):
    x = jnp.where(pl.program_id(0) < n_a, xa_ref[...], xb_ref[...])
    h_ref[...] = _rms(x, g_ref[...]).astype(h_ref.dtype)


def _norm_kernel(x_ref, g_ref, h_ref):
    h_ref[...] = _rms(x_ref[...], g_ref[...]).astype(h_ref.dtype)


def _addnorm_kernel(x_ref, y_ref, g_ref, s_ref, h_ref):
    x = x_ref[...] + y_ref[...]
    s_ref[...] = x
    h_ref[...] = _rms(x, g_ref[...]).astype(h_ref.dtype)


def _addnorm_final_kernel(x_ref, y_ref, g_ref, h_ref):
    h_ref[...] = _rms(x_ref[...] + y_ref[...], g_ref[...]).astype(h_ref.dtype)


def _two_source_specs(ta, tm, width):
    n_a = ta // tm
    spec_a = pl.BlockSpec((tm, width), lambda i, *_: (jnp.minimum(i, n_a - 1), 0))
    spec_b = pl.BlockSpec((tm, width), lambda i, *_: (jnp.maximum(i - n_a, 0), 0))
    return n_a, spec_a, spec_b


def rmsnorm_rows2(xa, xb, g, out_dtype):
    ta, d = xa.shape
    t = ta + xb.shape[0]
    tm = _common_tile(NORM_ROWS, ta, t)
    n_a, spec_a, spec_b = _two_source_specs(ta, tm, d)
    return pl.pallas_call(
        functools.partial(_norm2_kernel, n_a),
        out_shape=jax.ShapeDtypeStruct((t, d), out_dtype),
        grid=(t // tm,),
        in_specs=[spec_a, spec_b, pl.BlockSpec((1, d), lambda i: (0, 0))],
        out_specs=pl.BlockSpec((tm, d), lambda i: (i, 0)),
        compiler_params=_params(("parallel",)),
        name="rmsnorm2",
    )(xa, xb, g.reshape(1, d))


def rmsnorm_rows(x, g, out_dtype):
    t, d = x.shape
    tm = _row_tile(t, NORM_ROWS)
    row = pl.BlockSpec((tm, d), lambda i: (i, 0))
    return pl.pallas_call(
        _norm_kernel,
        out_shape=jax.ShapeDtypeStruct((t, d), out_dtype),
        grid=(t // tm,),
        in_specs=[row, pl.BlockSpec((1, d), lambda i: (0, 0))],
        out_specs=row,
        compiler_params=_params(("parallel",)),
        name="rmsnorm",
    )(x, g.reshape(1, d))


def add_rmsnorm_rows(x, y, g, out_dtype):
    t, d = x.shape
    tm = _row_tile(t, NORM_ROWS)
    row = pl.BlockSpec((tm, d), lambda i: (i, 0))
    return pl.pallas_call(
        _addnorm_kernel,
        out_shape=(jax.ShapeDtypeStruct((t, d), F32), jax.ShapeDtypeStruct((t, d), out_dtype)),
        grid=(t // tm,),
        in_specs=[row, row, pl.BlockSpec((1, d), lambda i: (0, 0))],
        out_specs=(row, row),
        compiler_params=_params(("parallel",)),
        name="add_rmsnorm",
    )(x, y, g.reshape(1, d))


def add_rmsnorm_final(x, y, g, row_start, n_rows):
    d = x.shape[1]
    tm = _row_tile(n_rows, NORM_ROWS)
    off = row_start // tm
    src = pl.BlockSpec((tm, d), lambda i: (i + off, 0))
    return pl.pallas_call(
        _addnorm_final_kernel,
        out_shape=jax.ShapeDtypeStruct((n_rows, d), F32),
        grid=(n_rows // tm,),
        in_specs=[src, src, pl.BlockSpec((1, d), lambda i: (0, 0))],
        out_specs=pl.BlockSpec((tm, d), lambda i: (i, 0)),
        compiler_params=_params(("parallel",)),
        name="add_rmsnorm_final",
    )(x, y, g.reshape(1, d))


def _mm_kernel(a_ref, b_ref, o_ref):
    o_ref[...] = jnp.dot(a_ref[...], b_ref[...], preferred_element_type=F32)


def _mm_slab_kernel(a_ref, b_ref, o_ref):
    acc = jnp.dot(a_ref[...], b_ref[...], preferred_element_type=F32)
    for j in range(o_ref.shape[0]):
        o_ref[j] = acc[:, j * LANES:(j + 1) * LANES]


def matmul(a, b, slab_out=False):
    t, k = a.shape
    n = b.shape[1]
    tm, tn = _row_tile(t, MM_ROWS), MM_COLS
    grid = (t // tm, n // tn)
    in_specs = [pl.BlockSpec((tm, k), lambda i, j: (i, 0)), pl.BlockSpec((k, tn), lambda i, j: (0, j))]
    if slab_out:
        return pl.pallas_call(
            _mm_slab_kernel,
            out_shape=jax.ShapeDtypeStruct((n // LANES, t, LANES), F32),
            grid=grid,
            in_specs=in_specs,
            out_specs=pl.BlockSpec((tn // LANES, tm, LANES), lambda i, j: (j, i, 0)),
            compiler_params=_params(("parallel", "parallel")),
            name="matmul_slab",
        )(a, b)
    return pl.pallas_call(
        _mm_kernel,
        out_shape=jax.ShapeDtypeStruct((t, n), F32),
        grid=grid,
        in_specs=in_specs,
        out_specs=pl.BlockSpec((tm, tn), lambda i, j: (i, j)),
        compiler_params=_params(("parallel", "parallel")),
        name="matmul",
    )(a, b)


def _mm2_res_kernel(a1_ref, a2_ref, b1_ref, b2_ref, r_ref, o_ref):
    acc = jnp.dot(a1_ref[...], b1_ref[...], preferred_element_type=F32)
    acc = acc + jnp.dot(a2_ref[...], b2_ref[...], preferred_element_type=F32)
    o_ref[...] = acc + r_ref[...]


def _mm2_res2_kernel(n_a, a1_ref, a2_ref, b1_ref, b2_ref, ra_ref, rb_ref, o_ref):
    acc = jnp.dot(a1_ref[...], b1_ref[...], preferred_element_type=F32)
    acc = acc + jnp.dot(a2_ref[...], b2_ref[...], preferred_element_type=F32)
    o_ref[...] = acc + jnp.where(pl.program_id(0) < n_a, ra_ref[...], rb_ref[...])


def matmul2_residual(a1, a2, b, res, res_b=None):
    t, kh = a1.shape
    n = b.shape[1]
    tm = _row_tile(t, MM_ROWS) if res_b is None else _common_tile(MM_ROWS, res.shape[0], t)
    tn = MM_COLS
    specs = [
        pl.BlockSpec((tm, kh), lambda i, j: (i, 0)),
        pl.BlockSpec((tm, kh), lambda i, j: (i, 0)),
        pl.BlockSpec((kh, tn), lambda i, j: (0, j)),
        pl.BlockSpec((kh, tn), lambda i, j: (1, j)),
    ]
    if res_b is None:
        body = _mm2_res_kernel
        specs.append(pl.BlockSpec((tm, tn), lambda i, j: (i, j)))
        args = (a1, a2, b, b, res)
    else:
        n_a = res.shape[0] // tm
        body = functools.partial(_mm2_res2_kernel, n_a)
        specs.append(pl.BlockSpec((tm, tn), lambda i, j: (jnp.minimum(i, n_a - 1), j)))
        specs.append(pl.BlockSpec((tm, tn), lambda i, j: (jnp.maximum(i - n_a, 0), j)))
        args = (a1, a2, b, b, res, res_b)
    return pl.pallas_call(
        body,
        out_shape=jax.ShapeDtypeStruct((t, n), F32),
        grid=(t // tm, n // tn),
        in_specs=specs,
        out_specs=pl.BlockSpec((tm, tn), lambda i, j: (i, j)),
        compiler_params=_params(("parallel", "parallel")),
        name="matmul2_residual",
    )(*args)


def _hgrn_masks(cb, width, reverse):
    ti = lax.broadcasted_iota(jnp.int32, (cb, width), 0)
    ji = lax.broadcasted_iota(jnp.int32, (cb, width), 1) % HEAD_DIM
    same = (ti // HGRN_SUB) == (ji // HGRN_SUB)
    masks = [jnp.logical_and(same, (ji >= ti) if reverse else (ji <= ti))]
    hs = HGRN_SUB
    while 2 * hs <= cb:
        grp = (ti // (2 * hs)) == (ji // (2 * hs))
        t_hi = (ti % (2 * hs)) >= hs
        j_hi = (ji % (2 * hs)) >= hs
        if reverse:
            sel = jnp.logical_and(jnp.logical_not(t_hi), j_hi)
        else:
            sel = jnp.logical_and(t_hi, jnp.logical_not(j_hi))
        masks.append(jnp.logical_and(grp, sel))
        hs *= 2
    t1 = lax.broadcasted_iota(jnp.int32, (cb, cb), 0)
    j1 = lax.broadcasted_iota(jnp.int32, (cb, cb), 1)
    tri = ((j1 >= t1) if reverse else (j1 <= t1)).astype(BF16)
    return masks, tri


def _ref_rows(b, group, idx):
    cb, w = b.shape
    g = b.reshape(cb // group, group, w)
    r = jnp.broadcast_to(g[:, idx:idx + 1, :], g.shape)
    return r.reshape(cb, w)


def _cumsum_rows(tri, a):
    w = a.shape[1]
    a1 = a.astype(BF16)
    r1 = a - a1.astype(F32)
    a2 = r1.astype(BF16)
    a3 = (r1 - a2.astype(F32)).astype(BF16)
    out = jnp.dot(tri, jnp.concatenate([a1, a2, a3], axis=1), preferred_element_type=F32)
    return out[:, :w] + out[:, w:2 * w] + out[:, 2 * w:]


def _heads(x):
    return [x[:, h * HEAD_DIM:(h + 1) * HEAD_DIM] for h in range(x.shape[1] // HEAD_DIM)]


def _hgrn_group(qz, fz, v, lb, st_ref, first_head, masks, tri, reverse):
    cb = qz.shape[0]
    q = qz * _sigmoid(qz)
    f = lb + (1.0 - lb) * _sigmoid(fz)
    a = jnp.log(f) * LOG2_E
    k = 1.0 - f
    b = _cumsum_rows(tri, a)
    b_tot = b[0:1, :] if reverse else b[cb - 1:cb, :]

    def scores(eq, ek, mask):
        qs, ks = _heads((q * eq).astype(BF16)), _heads((k * ek).astype(BF16))
        s = jnp.concatenate([_dot_nt(qh, kh) for qh, kh in zip(qs, ks)], axis=1)
        return jnp.where(mask, s, 0.0)

    r0 = _ref_rows(b, HGRN_SUB, HGRN_SUB // 2 if reverse else HGRN_SUB // 2 - 1)
    att = scores(jnp.exp2(b - r0), jnp.exp2(r0 - b), masks[0])
    hs = HGRN_SUB
    for mask in masks[1:]:
        e = jnp.exp2(-jnp.abs(b - _ref_rows(b, 2 * hs, hs if reverse else hs - 1)))
        att = att + scores(e, e, mask)
        hs *= 2
    att_h = _heads(att.astype(BF16))
    v_h = _heads(v)
    q_dec = _heads((q * jnp.exp2(b)).astype(BF16))
    k_dec = _heads((k * jnp.exp2(b_tot - b)).astype(BF16))
    decay = _heads(jnp.exp2(b_tot))
    outs = []
    for h in range(len(v_h)):
        st = st_ref[first_head + h]
        o = jnp.dot(att_h[h], v_h[h].astype(BF16), preferred_element_type=F32)
        outs.append(o + _dot_nt(q_dec[h], st.astype(BF16)))
        st_ref[first_head + h] = st * decay[h] + jnp.dot(v_h[h].T.astype(BF16), k_dec[h],
                                                         preferred_element_type=F32)
    return jnp.concatenate(outs, axis=1)


def _lower_bound(lbl, layer):
    if layer == 0:
        return jnp.zeros((1, lbl.shape[1]), F32)
    m = jnp.max(lbl, axis=0, keepdims=True)
    e = jnp.exp(lbl - m)
    sm = e / jnp.sum(e, axis=0, keepdims=True)
    return jnp.sum(sm[1:layer + 1, :], axis=0, keepdims=True)


def _hgrn_fwd_kernel(layer, q_ref, f_ref, i_ref, lbl_ref, o_ref, st_ref):
    @pl.when(pl.program_id(1) == 0)
    def _():
        st_ref[...] = jnp.zeros_like(st_ref)

    gw = HGRN_GROUP * HEAD_DIM
    masks, tri = _hgrn_masks(q_ref.shape[0], gw, False)
    lb = _lower_bound(lbl_ref[...], layer)
    for g in range(N_HGRN_HEADS // HGRN_GROUP):
        sl = slice(g * gw, (g + 1) * gw)
        o_ref[:, sl] = _hgrn_group(q_ref[:, sl], f_ref[:, sl], i_ref[:, sl], lb[:, sl], st_ref,
                                   g * HGRN_GROUP, masks, tri, False)


def _hgrn_bwd_kernel(layer, q_ref, f_ref, i_ref, g_ref, of_ref, lbl_ref, nw_ref, o_ref, st_ref):
    @pl.when(pl.program_id(1) == 0)
    def _():
        st_ref[...] = jnp.zeros_like(st_ref)

    gw = HGRN_GROUP * HEAD_DIM
    masks, tri = _hgrn_masks(q_ref.shape[0], gw, True)
    lb = _lower_bound(lbl_ref[...], layer)
    for g in range(N_HGRN_HEADS // HGRN_GROUP):
        sl = slice(g * gw, (g + 1) * gw)
        o = _hgrn_group(q_ref[:, sl], f_ref[:, sl], i_ref[:, sl], lb[:, sl], st_ref,
                        g * HGRN_GROUP, masks, tri, True) + of_ref[:, sl]
        o = jnp.concatenate([oh * lax.rsqrt(jnp.mean(oh * oh, axis=-1, keepdims=True) + EPS)
                             for oh in _heads(o)], axis=1) * nw_ref[:, sl]
        gz = g_ref[:, sl]
        o_ref[:, sl] = (o * (gz * _sigmoid(gz))).astype(o_ref.dtype)


def hgrn2(p, lb_logits_t, norm_w, layer, n_seq, seq_len):
    t = p.shape[0]
    w = N_HGRN_HEADS * HEAD_DIM
    cb = HGRN_BLOCK
    nb = seq_len // cb
    depth = lb_logits_t.shape[1]

    def col(j, reverse):
        if reverse:
            return pl.BlockSpec((cb, w), lambda s, c: (s * nb + nb - 1 - c, j))
        return pl.BlockSpec((cb, w), lambda s, c: (s * nb + c, j))

    def lbl(d):
        return pl.BlockSpec((None, depth, w), lambda s, c: (d, 0, 0))

    scratch = [pltpu.VMEM((N_HGRN_HEADS, HEAD_DIM, HEAD_DIM), F32)]
    o_f = pl.pallas_call(
        functools.partial(_hgrn_fwd_kernel, layer),
        out_shape=jax.ShapeDtypeStruct((t, w), F32),
        grid=(n_seq, nb),
        in_specs=[col(0, False), col(1, False), col(3, False), lbl(0)],
        out_specs=col(0, False),
        scratch_shapes=scratch,
        compiler_params=_params(("parallel", "arbitrary")),
        name="hgrn_fwd",
    )(p, p, p, lb_logits_t)
    return pl.pallas_call(
        functools.partial(_hgrn_bwd_kernel, layer),
        out_shape=jax.ShapeDtypeStruct((t, w), BF16),
        grid=(n_seq, nb),
        in_specs=[col(0, True), col(2, True), col(3, True), col(4, True), col(0, True), lbl(1),
                  pl.BlockSpec((1, w), lambda s, c: (0, 0))],
        out_specs=col(0, True),
        scratch_shapes=scratch,
        compiler_params=_params(("parallel", "arbitrary")),
        name="hgrn_bwd",
    )(p, p, p, p, o_f, lb_logits_t, norm_w.reshape(1, w))


def _rope(x, cos, sin_signed):
    return x * cos + pltpu.roll(x, HEAD_DIM // 2, axis=1) * sin_signed


def _attn_kernel(q_ref, kp_ref, kc_ref, kn_ref, vp_ref, vc_ref, vn_ref,
                 cp_ref, sp_ref, cc_ref, sc_ref, cn_ref, sn_ref, sink_ref, nw_ref, o_ref):
    n = pl.program_id(1)
    nb = pl.num_programs(1)
    tb = q_ref.shape[0]
    rows = GQA_REP * tb
    ri = lax.broadcasted_iota(jnp.int32, (rows, tb), 0) % tb
    ci = lax.broadcasted_iota(jnp.int32, (rows, tb), 1)
    head_of_row = lax.broadcasted_iota(jnp.int32, (rows, 1), 0) // tb
    m_prev = jnp.logical_and(ci >= ri, n > 0)
    m_next = jnp.logical_and(ci <= ri, n < nb - 1)
    scale = HEAD_DIM ** -0.5
    cc, sc = cc_ref[...], sc_ref[...]
    for g in range(N_KV_HEADS):
        sl = slice(g * HEAD_DIM, (g + 1) * HEAD_DIM)
        kcat = jnp.concatenate([
            _rope(kp_ref[:, sl], cp_ref[...], sp_ref[...]),
            _rope(kc_ref[:, sl], cc, sc),
            _rope(kn_ref[:, sl], cn_ref[...], sn_ref[...]),
        ], axis=0).astype(BF16)
        vcat = jnp.concatenate([vp_ref[:, sl], vc_ref[:, sl], vn_ref[:, sl]], axis=0).astype(BF16)
        heads = [g * GQA_REP + r for r in range(GQA_REP)]
        q = jnp.concatenate([_rope(q_ref[:, h * HEAD_DIM:(h + 1) * HEAD_DIM], cc, sc) for h in heads],
                            axis=0).astype(BF16)
        s = _dot_nt(q, kcat) * scale
        s = jnp.concatenate([jnp.where(m_prev, s[:, :tb], NEG_INF), s[:, tb:2 * tb],
                             jnp.where(m_next, s[:, 2 * tb:], NEG_INF)], axis=1)
        sink = jnp.zeros((rows, 1), F32)
        for r, h in enumerate(heads):
            sink = jnp.where(head_of_row == r, sink_ref[h], sink)
        m = jnp.maximum(jnp.max(s, axis=-1, keepdims=True), sink)
        e = jnp.exp(s - m)
        den = jnp.sum(e, axis=-1, keepdims=True) + jnp.exp(sink - m)
        o = jnp.dot(e.astype(BF16), vcat, preferred_element_type=F32) / den
        o = o * lax.rsqrt(jnp.mean(o * o, axis=-1, keepdims=True) + EPS)
        for r, h in enumerate(heads):
            hs = slice(h * HEAD_DIM, (h + 1) * HEAD_DIM)
            o_ref[:, hs] = (o[r * tb:(r + 1) * tb, :] * nw_ref[:, hs]).astype(o_ref.dtype)


def window_attention(p, cos, sin_signed, sink, norm_w, n_seq, seq_len, q_col, k_col, v_col):
    t = p.shape[0]
    tb = ATTN_BLOCK
    nb = seq_len // tb
    qw = N_ATTN_HEADS * HEAD_DIM
    kw = N_KV_HEADS * HEAD_DIM

    def prev(n):
        return jnp.maximum(n - 1, 0)

    def nxt(n):
        return jnp.minimum(n + 1, nb - 1)

    def same(n):
        return n

    def kv(colblk, f):
        return pl.BlockSpec((tb, kw), lambda s, n: (s * nb + f(n), colblk))

    def tab(f):
        return pl.BlockSpec((tb, HEAD_DIM), lambda s, n: (f(n), 0))

    qspec = pl.BlockSpec((tb, qw), lambda s, n: (s * nb + n, q_col // qw))
    kb, vb = k_col // kw, v_col // kw
    return pl.pallas_call(
        _attn_kernel,
        out_shape=jax.ShapeDtypeStruct((t, qw), BF16),
        grid=(n_seq, nb),
        in_specs=[qspec, kv(kb, prev), kv(kb, same), kv(kb, nxt), kv(vb, prev), kv(vb, same), kv(vb, nxt),
                  tab(prev), tab(prev), tab(same), tab(same), tab(nxt), tab(nxt),
                  pl.BlockSpec(memory_space=pltpu.SMEM),
                  pl.BlockSpec((1, qw), lambda s, n: (0, 0))],
        out_specs=pl.BlockSpec((tb, qw), lambda s, n: (s * nb + n, 0)),
        compiler_params=_params(("parallel", "parallel")),
        name="window_attention",
    )(p, p, p, p, p, p, p, cos, sin_signed, cos, sin_signed, cos, sin_signed, sink, norm_w.reshape(1, qw))


def rope_tables(seq_len):
    inv_freq = ROPE_THETA ** (-jnp.arange(0, HEAD_DIM, 2, dtype=F32) / HEAD_DIM)
    ang = jnp.arange(seq_len, dtype=F32)[:, None] * inv_freq[None, :]
    ang = jnp.concatenate([ang, ang], axis=-1)
    sign = jnp.concatenate([-jnp.ones((HEAD_DIM // 2,), F32), jnp.ones((HEAD_DIM // 2,), F32)])
    return jnp.cos(ang), jnp.sin(ang) * sign[None, :]


assert PEER_TOPK == 2 * SUBLANES
_PAIR_LIMIT = [PEER_TOPK // (a + 1) for a in range(PEER_TOPK)]
_N_MID = PEER_TOPK // 2 - 1
_CAND_ROWS = PEER_TOPK + SUBLANES * _N_MID + SUBLANES
assert all(lim <= SUBLANES for lim in _PAIR_LIMIT[1:]) and all(lim == 1 for lim in _PAIR_LIMIT[_N_MID + 1:])


def _extract_top(s, iota, k_out):
    n = s.shape[0]
    vals, idxs = [], []
    for _ in range(k_out):
        m = jnp.max(s, axis=0, keepdims=True)
        idx = jnp.min(jnp.where(s == m, iota, float(n)), axis=0, keepdims=True)
        s = jnp.where(iota == idx, -jnp.inf, s)
        vals.append(m)
        idxs.append(idx)
    return vals, idxs


def _route_kernel(qy_ref, sk_ref, i_ref, j_ref, g_ref, ti, tj, tg):
    tm = qy_ref.shape[1]
    kio = lax.broadcasted_iota(jnp.int32, (N_KEYS, LANES), 0).astype(F32)
    pio = lax.broadcasted_iota(jnp.int32, (_CAND_ROWS, LANES), 0).astype(F32)
    sub = lax.broadcasted_iota(jnp.int32, (SUBLANES, LANES), 0)

    def head(h, carry):
        row = pl.multiple_of(h * PEER_TOPK, PEER_TOPK)
        for half in range(tm // LANES):
            tok = slice(half * LANES, (half + 1) * LANES)
            stage1 = []
            for c in range(2):
                keys = sk_ref[h, c].astype(BF16)
                q = qy_ref[2 * h + c, tok, :].astype(BF16)
                stage1.append(_extract_top(_dot_nt(keys, q), kio, PEER_TOPK))
            (v0, i0), (v1, i1) = stage1
            v1_lo = jnp.concatenate(v1[:SUBLANES], axis=0)
            groups = [v0[0] + jnp.concatenate(v1, axis=0)]
            for a in range(1, _N_MID + 1):
                groups.append(jnp.where(sub < _PAIR_LIMIT[a], v0[a] + v1_lo, -jnp.inf))
            groups.append(jnp.concatenate(v0[_N_MID + 1:], axis=0) + v1[0])
            top_v, pos = _extract_top(jnp.concatenate(groups, axis=0), pio, PEER_TOPK)
            top_v, pos = jnp.concatenate(top_v, axis=0), jnp.concatenate(pos, axis=0)
            mid = jnp.floor((pos - PEER_TOPK) * (1.0 / SUBLANES))
            last = float(PEER_TOPK + SUBLANES * _N_MID)
            a_sel = jnp.where(pos < PEER_TOPK, 0.0, jnp.where(pos < last, 1.0 + mid, pos - (last - _N_MID - 1)))
            b_sel = jnp.where(pos < PEER_TOPK, pos,
                              jnp.where(pos < last, pos - PEER_TOPK - SUBLANES * mid, 0.0))
            ki = jnp.zeros_like(pos)
            kj = jnp.zeros_like(pos)
            for a in range(PEER_TOPK):
                ki = ki + jnp.where(a_sel == float(a), i0[a], 0.0)
                kj = kj + jnp.where(b_sel == float(a), i1[a], 0.0)
            e = jnp.exp(top_v - top_v[0:1, :])
            ti[pl.ds(row, PEER_TOPK), tok] = ki
            tj[pl.ds(row, PEER_TOPK), tok] = kj
            tg[pl.ds(row, PEER_TOPK), tok] = e / jnp.sum(e, axis=0, keepdims=True)
        return carry

    lax.fori_loop(0, PEER_HEADS, head, 0, unroll=ROUTE_UNROLL)
    i_ref[...] = ti[...].T
    j_ref[...] = tj[...].T
    g_ref[...] = tg[...].T


def peer_route(qy_slab, sub_keys):
    t = qy_slab.shape[1]
    tm = _row_tile(t, ROUTE_TILE)
    slots = PEER_HEADS * PEER_TOPK
    out = jax.ShapeDtypeStruct((t, slots), F32)
    ospec = pl.BlockSpec((tm, slots), lambda i: (i, 0))
    full = pltpu.VMEM((slots, tm), F32)
    return pl.pallas_call(
        _route_kernel,
        out_shape=(out, out, out),
        grid=(t // tm,),
        in_specs=[pl.BlockSpec((2 * PEER_HEADS, tm, HEAD_DIM), lambda i: (0, i, 0)),
                  pl.BlockSpec(sub_keys.shape, lambda i: (0, 0, 0, 0))],
        out_specs=(ospec, ospec, ospec),
        scratch_shapes=[full, full, full],
        compiler_params=_params(("parallel",)),
        name="peer_route",
    )(qy_slab, sub_keys)


def _coef_kernel(i_ref, j_ref, g_ref, c_ref, stage):
    tm = i_ref.shape[0]
    sub = lax.broadcasted_iota(jnp.int32, (N_KEYS, i_ref.shape[1]), 0).astype(F32)

    def group(gi, carry):
        t0 = pl.multiple_of(gi * COEF_GROUP, COEF_GROUP)
        ib = i_ref[pl.ds(t0, COEF_GROUP), :]
        jb = j_ref[pl.ds(t0, COEF_GROUP), :]
        gb = g_ref[pl.ds(t0, COEF_GROUP), :]
        for t in range(COEF_GROUP):
            ptg = jnp.where(sub == ib[t:t + 1, :], gb[t:t + 1, :], 0.0).astype(BF16)
            qt = jnp.where(sub == jb[t:t + 1, :], 1.0, 0.0).astype(BF16)
            stage[t * COEF_PITCH:t * COEF_PITCH + N_KEYS, :] = _dot_nt(ptg, qt)
        for i in range(N_KEYS):
            c_ref[i, pl.ds(t0, COEF_GROUP), :] = stage[pl.ds(i, COEF_GROUP, stride=COEF_PITCH), :].astype(c_ref.dtype)
        return carry

    lax.fori_loop(0, tm // COEF_GROUP, group, 0)


def peer_coefficients(ki, kj, gate):
    t, slots = ki.shape
    tm = _row_tile(t, COEF_TILE)
    spec = pl.BlockSpec((tm, slots), lambda i: (i, 0))
    return pl.pallas_call(
        _coef_kernel,
        out_shape=jax.ShapeDtypeStruct((N_KEYS, t, N_KEYS), BF16),
        grid=(t // tm,),
        in_specs=[spec, spec, spec],
        out_specs=pl.BlockSpec((N_KEYS, tm, N_KEYS), lambda i: (0, i, 0)),
        scratch_shapes=[pltpu.VMEM((COEF_GROUP * COEF_PITCH, N_KEYS), F32)],
        compiler_params=_params(("parallel",)),
        name="peer_coefficients",
    )(ki, kj, gate)


def _gelu(x):
    return 0.5 * x * (1.0 + lax.erf(x * 0.7071067811865476))


def _peer_kernel(x_ref, u_ref, v_ref, c_ref, o_ref):
    @pl.when(pl.program_id(1) == 0)
    def _():
        o_ref[...] = jnp.zeros_like(o_ref)

    act = _gelu(_dot_nt(x_ref[...], u_ref[...]))
    w = jnp.concatenate([(c_ref[s].astype(F32) * act[:, s * N_KEYS:(s + 1) * N_KEYS]).astype(BF16)
                         for s in range(c_ref.shape[0])], axis=1)
    o_ref[...] += jnp.dot(w, v_ref[...], preferred_element_type=F32)


def peer_dense(x, u, v, coef):
    t, d = x.shape
    e = u.shape[0]
    tm, te = _row_tile(t, PEER_ROWS), PEER_EXPERTS
    return pl.pallas_call(
        _peer_kernel,
        out_shape=jax.ShapeDtypeStruct((t, d), F32),
        grid=(t // tm, e // te),
        in_specs=[pl.BlockSpec((tm, d), lambda i, j: (i, 0)),
                  pl.BlockSpec((te, d), lambda i, j: (j, 0)),
                  pl.BlockSpec((te, d), lambda i, j: (j, 0)),
                  pl.BlockSpec((te // N_KEYS, tm, N_KEYS), lambda i, j: (j, i, 0))],
        out_specs=pl.BlockSpec((tm, d), lambda i, j: (i, 0)),
        compiler_params=_params(("parallel", "arbitrary")),
        name="peer_dense",
    )(x, u, v, coef)


def kernel(x_prompt, x_sample, w_in, w_out, norm_mix, norm_ffn, lb_logits, hgrn_norm, attn_norm, attn_sink,
           peer_query, peer_sub_keys, peer_u, peer_v, final_norm):
    depth = w_in.shape[0]
    d_model = x_prompt.shape[-1]
    seq_len = x_prompt.shape[1]
    assert x_sample.shape[1] == seq_len, "prompt and sample sequences are stacked and must share a length"
    n_seq = x_prompt.shape[0] + x_sample.shape[0]
    xp = x_prompt.reshape(-1, d_model)
    xs = x_sample.reshape(-1, d_model)
    t_prompt, t_sample = xp.shape[0], xs.shape[0]
    hw = N_HGRN_HEADS * HEAD_DIM
    q_col = 5 * hw
    k_col = q_col + N_ATTN_HEADS * HEAD_DIM
    v_col = k_col + N_KV_HEADS * HEAD_DIM

    cos, sin_signed = rope_tables(seq_len)
    lbl_t = jnp.transpose(lb_logits.astype(F32), (1, 0, 2))

    h = rmsnorm_rows2(xp, xs, norm_mix[0], BF16)
    x = None
    for l in range(depth):
        p = matmul(h, cast_bf16(w_in[l]))
        o_h = hgrn2(p, lbl_t, hgrn_norm[l], l, n_seq, seq_len)
        o_a = window_attention(p, cos, sin_signed, attn_sink[l].astype(F32), attn_norm[l], n_seq, seq_len,
                               q_col, k_col, v_col)
        if l == 0:
            x = matmul2_residual(o_h, o_a, cast_bf16(w_out[l]), xp, xs)
        else:
            x = matmul2_residual(o_h, o_a, cast_bf16(w_out[l]), x)
        h2 = rmsnorm_rows(x, norm_ffn[l], BF16)
        qy = matmul(h2, cast_bf16(peer_query[l]), slab_out=True)
        ki, kj, gate = peer_route(qy, peer_sub_keys[l])
        coef = peer_coefficients(ki, kj, gate)
        po = peer_dense(h2, cast_bf16(peer_u[l]), cast_bf16(peer_v[l]), coef)
        if l + 1 < depth:
            x, h = add_rmsnorm_rows(x, po, norm_mix[l + 1], BF16)
    y_prompt = add_rmsnorm_final(x, po, final_norm, 0, t_prompt)
    y_sample = add_rmsnorm_final(x, po, final_norm, t_prompt, t_sample)
    return (y_prompt.reshape(x_prompt.shape), y_sample.reshape(x_sample.shape))
```

```python
import functools

import jax
import jax.numpy as jnp
from jax import lax
from jax.experimental import pallas as pl
from jax.experimental.pallas import tpu as pltpu

F32 = jnp.float32
BF16 = jnp.bfloat16

LANES = 128
SUBLANES = 8
HEAD_DIM = LANES
N_HGRN_HEADS = 16
N_ATTN_HEADS = 16
N_KV_HEADS = 4
GQA_REP = N_ATTN_HEADS // N_KV_HEADS
WINDOW = 128
ROPE_THETA = 10000.0
HGRN_SUB = 16
N_KEYS = 128
PEER_HEADS = 8
PEER_TOPK = 16
EPS = 1e-6
NEG_INF = -1e30
LOG2_E = 1.4426950408889634

VMEM_LIMIT = 56 * 1024 * 1024

NORM_ROWS = 256
MM_ROWS = 1024
MM_COLS = 512
CAST_BLOCK = 1024
HGRN_BLOCK = 128
HGRN_GROUP = 4
ATTN_BLOCK = WINDOW
ROUTE_TILE = 256
ROUTE_UNROLL = 4
COEF_TILE = 256
COEF_GROUP = 16
COEF_PITCH = N_KEYS + SUBLANES
COEF_UNROLL = 4
PEER_ROWS = 512
PEER_EXPERTS = 1024


def _params(sem):
    return pltpu.CompilerParams(dimension_semantics=sem, vmem_limit_bytes=VMEM_LIMIT)


def _sigmoid(x):
    return 1.0 / (1.0 + jnp.exp(-x))


def _dot_nt(a, b):
    return lax.dot_general(a, b, (((1,), (1,)), ((), ())), preferred_element_type=F32)


def _row_tile(t, want):
    return want if t % want == 0 else t


def _common_tile(want, *sizes):
    while any(n % want for n in sizes):
        want //= 2
    return want


def _rms(x, g):
    return x * lax.rsqrt(jnp.mean(x * x, axis=-1, keepdims=True) + EPS) * g


def _cast_kernel(w_ref, o_ref):
    o_ref[...] = w_ref[...].astype(o_ref.dtype)


def cast_bf16(w, layer):
    _, r, c = w.shape
    br, bc = _row_tile(r, CAST_BLOCK), _row_tile(c, CAST_BLOCK)
    return pl.pallas_call(
        _cast_kernel,
        out_shape=jax.ShapeDtypeStruct((r, c), BF16),
        grid=(r // br, c // bc),
        in_specs=[pl.BlockSpec((None, br, bc), lambda i, j: (layer, i, j))],
        out_specs=pl.BlockSpec((br, bc), lambda i, j: (i, j)),
        compiler_params=_params(("parallel", "parallel")),
        name="cast_bf16",
    )(w)


def _norm2_kernel(n_a, xa_ref, xb_ref, g_ref, h_ref):
    x = jnp.where(pl.program_id(0) < n_a, xa_ref[...], xb_ref[...])
    h_ref[...] = _rms(x, g_ref[...]).astype(h_ref.dtype)


def _norm_kernel(x_ref, g_ref, h_ref):
    h_ref[...] = _rms(x_ref[...], g_ref[...]).astype(h_ref.dtype)


def _addnorm_kernel(x_ref, y_ref, g_ref, s_ref, h_ref):
    x = x_ref[...] + y_ref[...]
    s_ref[...] = x
    h_ref[...] = _rms(x, g_ref[...]).astype(h_ref.dtype)


def _addnorm_final_kernel(x_ref, y_ref, g_ref, h_ref):
    h_ref[...] = _rms(x_ref[...] + y_ref[...], g_ref[...]).astype(h_ref.dtype)


def _two_source_specs(ta, tm, width):
    n_a = ta // tm
    spec_a = pl.BlockSpec((tm, width), lambda i, *_: (jnp.minimum(i, n_a - 1), 0))
    spec_b = pl.BlockSpec((tm, width), lambda i, *_: (jnp.maximum(i - n_a, 0), 0))
    return n_a, spec_a, spec_b


def rmsnorm_rows2(xa, xb, g, out_dtype):
    ta, d = xa.shape
    t = ta + xb.shape[0]
    tm = _common_tile(NORM_ROWS, ta, t)
    n_a, spec_a, spec_b = _two_source_specs(ta, tm, d)
    return pl.pallas_call(
        functools.partial(_norm2_kernel, n_a),
        out_shape=jax.ShapeDtypeStruct((t, d), out_dtype),
        grid=(t // tm,),
        in_specs=[spec_a, spec_b, pl.BlockSpec((1, d), lambda i: (0, 0))],
        out_specs=pl.BlockSpec((tm, d), lambda i: (i, 0)),
        compiler_params=_params(("parallel",)),
        name="rmsnorm2",
    )(xa, xb, g.reshape(1, d))


def rmsnorm_rows(x, g, out_dtype):
    t, d = x.shape
    tm = _row_tile(t, NORM_ROWS)
    row = pl.BlockSpec((tm, d), lambda i: (i, 0))
    return pl.pallas_call(
        _norm_kernel,
        out_shape=jax.ShapeDtypeStruct((t, d), out_dtype),
        grid=(t // tm,),
        in_specs=[row, pl.BlockSpec((1, d), lambda i: (0, 0))],
        out_specs=row,
        compiler_params=_params(("parallel",)),
        name="rmsnorm",
    )(x, g.reshape(1, d))


def add_rmsnorm_rows(x, y, g, out_dtype):
    t, d = x.shape
    tm = _row_tile(t, NORM_ROWS)
    row = pl.BlockSpec((tm, d), lambda i: (i, 0))
    return pl.pallas_call(
        _addnorm_kernel,
        out_shape=(jax.ShapeDtypeStruct((t, d), F32), jax.ShapeDtypeStruct((t, d), out_dtype)),
        grid=(t // tm,),
        in_specs=[row, row, pl.BlockSpec((1, d), lambda i: (0, 0))],
        out_specs=(row, row),
        compiler_params=_params(("parallel",)),
        name="add_rmsnorm",
    )(x, y, g.reshape(1, d))


def add_rmsnorm_final(x, y, g, row_start, n_rows):
    d = x.shape[1]
    tm = _row_tile(n_rows, NORM_ROWS)
    off = row_start // tm
    src = pl.BlockSpec((tm, d), lambda i: (i + off, 0))
    return pl.pallas_call(
        _addnorm_final_kernel,
        out_shape=jax.ShapeDtypeStruct((n_rows, d), F32),
        grid=(n_rows // tm,),
        in_specs=[src, src, pl.BlockSpec((1, d), lambda i: (0, 0))],
        out_specs=pl.BlockSpec((tm, d), lambda i: (i, 0)),
        compiler_params=_params(("parallel",)),
        name="add_rmsnorm_final",
    )(x, y, g.reshape(1, d))


def _mm_kernel(a_ref, b_ref, o_ref):
    o_ref[...] = jnp.dot(a_ref[...], b_ref[...], preferred_element_type=F32)


def _mm_slab_kernel(a_ref, b_ref, o_ref):
    acc = jnp.dot(a_ref[...], b_ref[...], preferred_element_type=F32)
    for j in range(o_ref.shape[0]):
        o_ref[j] = acc[:, j * LANES:(j + 1) * LANES]


def matmul(a, b, slab_out=False):
    t, k = a.shape
    n = b.shape[1]
    tm, tn = _row_tile(t, MM_ROWS), MM_COLS
    grid = (t // tm, n // tn)
    in_specs = [pl.BlockSpec((tm, k), lambda i, j: (i, 0)), pl.BlockSpec((k, tn), lambda i, j: (0, j))]
    if slab_out:
        return pl.pallas_call(
            _mm_slab_kernel,
            out_shape=jax.ShapeDtypeStruct((n // LANES, t, LANES), F32),
            grid=grid,
            in_specs=in_specs,
            out_specs=pl.BlockSpec((tn // LANES, tm, LANES), lambda i, j: (j, i, 0)),
            compiler_params=_params(("parallel", "parallel")),
            name="matmul_slab",
        )(a, b)
    return pl.pallas_call(
        _mm_kernel,
        out_shape=jax.ShapeDtypeStruct((t, n), F32),
        grid=grid,
        in_specs=in_specs,
        out_specs=pl.BlockSpec((tm, tn), lambda i, j: (i, j)),
        compiler_params=_params(("parallel", "parallel")),
        name="matmul",
    )(a, b)


def _mm2_res_kernel(a1_ref, a2_ref, b1_ref, b2_ref, r_ref, o_ref):
    acc = jnp.dot(a1_ref[...], b1_ref[...], preferred_element_type=F32)
    acc = acc + jnp.dot(a2_ref[...], b2_ref[...], preferred_element_type=F32)
    o_ref[...] = acc + r_ref[...]


def _mm2_res2_kernel(n_a, a1_ref, a2_ref, b1_ref, b2_ref, ra_ref, rb_ref, o_ref):
    acc = jnp.dot(a1_ref[...], b1_ref[...], preferred_element_type=F32)
    acc = acc + jnp.dot(a2_ref[...], b2_ref[...], preferred_element_type=F32)
    o_ref[...] = acc + jnp.where(pl.program_id(0) < n_a, ra_ref[...], rb_ref[...])


def matmul2_residual(a1, a2, b, res, res_b=None):
    t, kh = a1.shape
    n = b.shape[1]
    tm = _row_tile(t, MM_ROWS) if res_b is None else _common_tile(MM_ROWS, res.shape[0], t)
    tn = MM_COLS
    specs = [
        pl.BlockSpec((tm, kh), lambda i, j: (i, 0)),
        pl.BlockSpec((tm, kh), lambda i, j: (i, 0)),
        pl.BlockSpec((kh, tn), lambda i, j: (0, j)),
        pl.BlockSpec((kh, tn), lambda i, j: (1, j)),
    ]
    if res_b is None:
        body = _mm2_res_kernel
        specs.append(pl.BlockSpec((tm, tn), lambda i, j: (i, j)))
        args = (a1, a2, b, b, res)
    else:
        n_a = res.shape[0] // tm
        body = functools.partial(_mm2_res2_kernel, n_a)
        specs.append(pl.BlockSpec((tm, tn), lambda i, j: (jnp.minimum(i, n_a - 1), j)))
        specs.append(pl.BlockSpec((tm, tn), lambda i, j: (jnp.maximum(i - n_a, 0), j)))
        args = (a1, a2, b, b, res, res_b)
    return pl.pallas_call(
        body,
        out_shape=jax.ShapeDtypeStruct((t, n), F32),
        grid=(t // tm, n // tn),
        in_specs=specs,
        out_specs=pl.BlockSpec((tm, tn), lambda i, j: (i, j)),
        compiler_params=_params(("parallel", "parallel")),
        name="matmul2_residual",
    )(*args)


def _hgrn_masks(cb, width, reverse):
    ti = lax.broadcasted_iota(jnp.int32, (cb, width), 0)
    ji = lax.broadcasted_iota(jnp.int32, (cb, width), 1) % HEAD_DIM
    same = (ti // HGRN_SUB) == (ji // HGRN_SUB)
    masks = [jnp.logical_and(same, (ji >= ti) if reverse else (ji <= ti))]
    hs = HGRN_SUB
    while 2 * hs <= cb:
        grp = (ti // (2 * hs)) == (ji // (2 * hs))
        t_hi = (ti % (2 * hs)) >= hs
        j_hi = (ji % (2 * hs)) >= hs
        if reverse:
            sel = jnp.logical_and(jnp.logical_not(t_hi), j_hi)
        else:
            sel = jnp.logical_and(t_hi, jnp.logical_not(j_hi))
        masks.append(jnp.logical_and(grp, sel))
        hs *= 2
    t1 = lax.broadcasted_iota(jnp.int32, (cb, cb), 0)
    j1 = lax.broadcasted_iota(jnp.int32, (cb, cb), 1)
    tri = ((j1 >= t1) if reverse else (j1 <= t1)).astype(BF16)
    return masks, tri


def _ref_rows(b, group, idx):
    cb, w = b.shape
    g = b.reshape(cb // group, group, w)
    r = jnp.broadcast_to(g[:, idx:idx + 1, :], g.shape)
    return r.reshape(cb, w)


def _cumsum_rows(tri, a):
    w = a.shape[1]
    a1 = a.astype(BF16)
    r1 = a - a1.astype(F32)
    a2 = r1.astype(BF16)
    a3 = (r1 - a2.astype(F32)).astype(BF16)
    out = jnp.dot(tri, jnp.concatenate([a1, a2, a3], axis=1), preferred_element_type=F32)
    return out[:, :w] + out[:, w:2 * w] + out[:, 2 * w:]


def _heads(x):
    return [x[:, h * HEAD_DIM:(h + 1) * HEAD_DIM] for h in range(x.shape[1] // HEAD_DIM)]


def _hgrn_group(qz, fz, v, lb, st_ref, first_head, masks, tri, reverse):
    cb = qz.shape[0]
    q = qz * _sigmoid(qz)
    f = lb + (1.0 - lb) * _sigmoid(fz)
    a = jnp.log(f) * LOG2_E
    k = 1.0 - f
    b = _cumsum_rows(tri, a)
    b_tot = b[0:1, :] if reverse else b[cb - 1:cb, :]

    def scores(eq, ek, mask):
        qs, ks = _heads((q * eq).astype(BF16)), _heads((k * ek).astype(BF16))
        s = jnp.concatenate([_dot_nt(qh, kh) for qh, kh in zip(qs, ks)], axis=1)
        return jnp.where(mask, s, 0.0)

    r0 = _ref_rows(b, HGRN_SUB, HGRN_SUB // 2 if reverse else HGRN_SUB // 2 - 1)
    att = scores(jnp.exp2(b - r0), jnp.exp2(r0 - b), masks[0])
    hs = HGRN_SUB
    for mask in masks[1:]:
        e = jnp.exp2(-jnp.abs(b - _ref_rows(b, 2 * hs, hs if reverse else hs - 1)))
        att = att + scores(e, e, mask)
        hs *= 2
    att_h = _heads(att.astype(BF16))
    v_h = _heads(v)
    q_dec = _heads((q * jnp.exp2(b)).astype(BF16))
    k_dec = _heads((k * jnp.exp2(b_tot - b)).astype(BF16))
    decay = _heads(jnp.exp2(b_tot))
    outs = []
    for h in range(len(v_h)):
        st = st_ref[first_head + h]
        o = jnp.dot(att_h[h], v_h[h].astype(BF16), preferred_element_type=F32)
        outs.append(o + _dot_nt(q_dec[h], st.astype(BF16)))
        st_ref[first_head + h] = st * decay[h] + jnp.dot(v_h[h].T.astype(BF16), k_dec[h],
                                                         preferred_element_type=F32)
    return jnp.concatenate(outs, axis=1)


def _lower_bound(lbl, layer):
    if layer == 0:
        return jnp.zeros((1, lbl.shape[1]), F32)
    m = jnp.max(lbl, axis=0, keepdims=True)
    e = jnp.exp(lbl - m)
    sm = e / jnp.sum(e, axis=0, keepdims=True)
    return jnp.sum(sm[1:layer + 1, :], axis=0, keepdims=True)


def _hgrn_fwd_kernel(layer, q_ref, f_ref, i_ref, lbl_ref, o_ref, st_ref):
    @pl.when(pl.program_id(1) == 0)
    def _():
        st_ref[...] = jnp.zeros_like(st_ref)

    gw = HGRN_GROUP * HEAD_DIM
    masks, tri = _hgrn_masks(q_ref.shape[0], gw, False)
    lb = _lower_bound(lbl_ref[...], layer)
    for g in range(N_HGRN_HEADS // HGRN_GROUP):
        sl = slice(g * gw, (g + 1) * gw)
        o_ref[:, sl] = _hgrn_group(q_ref[:, sl], f_ref[:, sl], i_ref[:, sl], lb[:, sl], st_ref,
                                   g * HGRN_GROUP, masks, tri, False)


def _hgrn_bwd_kernel(layer, q_ref, f_ref, i_ref, g_ref, of_ref, lbl_ref, nw_ref, o_ref, st_ref):
    @pl.when(pl.program_id(1) == 0)
    def _():
        st_ref[...] = jnp.zeros_like(st_ref)

    gw = HGRN_GROUP * HEAD_DIM
    masks, tri = _hgrn_masks(q_ref.shape[0], gw, True)
    lb = _lower_bound(lbl_ref[...], layer)
    for g in range(N_HGRN_HEADS // HGRN_GROUP):
        sl = slice(g * gw, (g + 1) * gw)
        o = _hgrn_group(q_ref[:, sl], f_ref[:, sl], i_ref[:, sl], lb[:, sl], st_ref,
                        g * HGRN_GROUP, masks, tri, True) + of_ref[:, sl]
        o = jnp.concatenate([oh * lax.rsqrt(jnp.mean(oh * oh, axis=-1, keepdims=True) + EPS)
                             for oh in _heads(o)], axis=1) * nw_ref[:, sl]
        gz = g_ref[:, sl]
        o_ref[:, sl] = (o * (gz * _sigmoid(gz))).astype(o_ref.dtype)


def hgrn2(p, lb_logits_t, norm_w, layer, n_seq, seq_len):
    t = p.shape[0]
    w = N_HGRN_HEADS * HEAD_DIM
    cb = HGRN_BLOCK
    nb = seq_len // cb
    depth = lb_logits_t.shape[1]

    def col(j, reverse):
        if reverse:
            return pl.BlockSpec((cb, w), lambda s, c: (s * nb + nb - 1 - c, j))
        return pl.BlockSpec((cb, w), lambda s, c: (s * nb + c, j))

    def lbl(d):
        return pl.BlockSpec((None, depth, w), lambda s, c: (d, 0, 0))

    scratch = [pltpu.VMEM((N_HGRN_HEADS, HEAD_DIM, HEAD_DIM), F32)]
    o_f = pl.pallas_call(
        functools.partial(_hgrn_fwd_kernel, layer),
        out_shape=jax.ShapeDtypeStruct((t, w), F32),
        grid=(n_seq, nb),
        in_specs=[col(0, False), col(1, False), col(3, False), lbl(0)],
        out_specs=col(0, False),
        scratch_shapes=scratch,
        compiler_params=_params(("parallel", "arbitrary")),
        name="hgrn_fwd",
    )(p, p, p, lb_logits_t)
    return pl.pallas_call(
        functools.partial(_hgrn_bwd_kernel, layer),
        out_shape=jax.ShapeDtypeStruct((t, w), BF16),
        grid=(n_seq, nb),
        in_specs=[col(0, True), col(2, True), col(3, True), col(4, True), col(0, True), lbl(1),
                  pl.BlockSpec((1, w), lambda s, c: (0, 0))],
        out_specs=col(0, True),
        scratch_shapes=scratch,
        compiler_params=_params(("parallel", "arbitrary")),
        name="hgrn_bwd",
    )(p, p, p, p, o_f, lb_logits_t, norm_w.reshape(1, w))


def _rope(x, cos, sin_signed):
    return x * cos + pltpu.roll(x, HEAD_DIM // 2, axis=1) * sin_signed


def _attn_kernel(q_ref, kp_ref, kc_ref, kn_ref, vp_ref, vc_ref, vn_ref,
                 cp_ref, sp_ref, cc_ref, sc_ref, cn_ref, sn_ref, sink_ref, nw_ref, o_ref):
    n = pl.program_id(1)
    nb = pl.num_programs(1)
    tb = q_ref.shape[0]
    rows = GQA_REP * tb
    ri = lax.broadcasted_iota(jnp.int32, (rows, tb), 0) % tb
    ci = lax.broadcasted_iota(jnp.int32, (rows, tb), 1)
    head_of_row = lax.broadcasted_iota(jnp.int32, (rows, 1), 0) // tb
    m_prev = jnp.logical_and(ci >= ri, n > 0)
    m_next = jnp.logical_and(ci <= ri, n < nb - 1)
    scale = HEAD_DIM ** -0.5
    cc, sc = cc_ref[...], sc_ref[...]
    for g in range(N_KV_HEADS):
        sl = slice(g * HEAD_DIM, (g + 1) * HEAD_DIM)
        kcat = jnp.concatenate([
            _rope(kp_ref[:, sl], cp_ref[...], sp_ref[...]),
            _rope(kc_ref[:, sl], cc, sc),
            _rope(kn_ref[:, sl], cn_ref[...], sn_ref[...]),
        ], axis=0).astype(BF16)
        vcat = jnp.concatenate([vp_ref[:, sl], vc_ref[:, sl], vn_ref[:, sl]], axis=0).astype(BF16)
        heads = [g * GQA_REP + r for r in range(GQA_REP)]
        q = jnp.concatenate([_rope(q_ref[:, h * HEAD_DIM:(h + 1) * HEAD_DIM], cc, sc) for h in heads],
                            axis=0).astype(BF16)
        s = _dot_nt(q, kcat) * scale
        s = jnp.concatenate([jnp.where(m_prev, s[:, :tb], NEG_INF), s[:, tb:2 * tb],
                             jnp.where(m_next, s[:, 2 * tb:], NEG_INF)], axis=1)
        sink = jnp.zeros((rows, 1), F32)
        for r, h in enumerate(heads):
            sink = jnp.where(head_of_row == r, sink_ref[h], sink)
        m = jnp.maximum(jnp.max(s, axis=-1, keepdims=True), sink)
        e = jnp.exp(s - m)
        den = jnp.sum(e, axis=-1, keepdims=True) + jnp.exp(sink - m)
        o = jnp.dot(e.astype(BF16), vcat, preferred_element_type=F32) / den
        o = o * lax.rsqrt(jnp.mean(o * o, axis=-1, keepdims=True) + EPS)
        for r, h in enumerate(heads):
            hs = slice(h * HEAD_DIM, (h + 1) * HEAD_DIM)
            o_ref[:, hs] = (o[r * tb:(r + 1) * tb, :] * nw_ref[:, hs]).astype(o_ref.dtype)


def window_attention(p, cos, sin_signed, sink, norm_w, n_seq, seq_len, q_col, k_col, v_col):
    t = p.shape[0]
    tb = ATTN_BLOCK
    nb = seq_len // tb
    qw = N_ATTN_HEADS * HEAD_DIM
    kw = N_KV_HEADS * HEAD_DIM

    def prev(n):
        return jnp.maximum(n - 1, 0)

    def nxt(n):
        return jnp.minimum(n + 1, nb - 1)

    def same(n):
        return n

    def kv(colblk, f):
        return pl.BlockSpec((tb, kw), lambda s, n: (s * nb + f(n), colblk))

    def tab(f):
        return pl.BlockSpec((tb, HEAD_DIM), lambda s, n: (f(n), 0))

    qspec = pl.BlockSpec((tb, qw), lambda s, n: (s * nb + n, q_col // qw))
    kb, vb = k_col // kw, v_col // kw
    return pl.pallas_call(
        _attn_kernel,
        out_shape=jax.ShapeDtypeStruct((t, qw), BF16),
        grid=(n_seq, nb),
        in_specs=[qspec, kv(kb, prev), kv(kb, same), kv(kb, nxt), kv(vb, prev), kv(vb, same), kv(vb, nxt),
                  tab(prev), tab(prev), tab(same), tab(same), tab(nxt), tab(nxt),
                  pl.BlockSpec(memory_space=pltpu.SMEM),
                  pl.BlockSpec((1, qw), lambda s, n: (0, 0))],
        out_specs=pl.BlockSpec((tb, qw), lambda s, n: (s * nb + n, 0)),
        compiler_params=_params(("parallel", "parallel")),
        name="window_attention",
    )(p, p, p, p, p, p, p, cos, sin_signed, cos, sin_signed, cos, sin_signed, sink, norm_w.reshape(1, qw))


def rope_tables(seq_len):
    inv_freq = ROPE_THETA ** (-jnp.arange(0, HEAD_DIM, 2, dtype=F32) / HEAD_DIM)
    ang = jnp.arange(seq_len, dtype=F32)[:, None] * inv_freq[None, :]
    ang = jnp.concatenate([ang, ang], axis=-1)
    sign = jnp.concatenate([-jnp.ones((HEAD_DIM // 2,), F32), jnp.ones((HEAD_DIM // 2,), F32)])
    return jnp.cos(ang), jnp.sin(ang) * sign[None, :]


assert PEER_TOPK == 2 * SUBLANES
_PAIR_LIMIT = [PEER_TOPK // (a + 1) for a in range(PEER_TOPK)]
_N_MID = PEER_TOPK // 2 - 1
_CAND_ROWS = PEER_TOPK + SUBLANES * _N_MID + SUBLANES
assert all(lim <= SUBLANES for lim in _PAIR_LIMIT[1:]) and all(lim == 1 for lim in _PAIR_LIMIT[_N_MID + 1:])


def _extract_top(s, iota, k_out):
    n = s.shape[0]
    vals, idxs = [], []
    for _ in range(k_out):
        m = jnp.max(s, axis=0, keepdims=True)
        idx = jnp.min(jnp.where(s == m, iota, float(n)), axis=0, keepdims=True)
        s = jnp.where(iota == idx, -jnp.inf, s)
        vals.append(m)
        idxs.append(idx)
    return vals, idxs


def _route_kernel(qy_ref, sk_ref, i_ref, j_ref, g_ref, ti, tj, tg):
    tm = qy_ref.shape[1]
    kio = lax.broadcasted_iota(jnp.int32, (N_KEYS, LANES), 0).astype(F32)
    pio = lax.broadcasted_iota(jnp.int32, (_CAND_ROWS, LANES), 0).astype(F32)
    sub = lax.broadcasted_iota(jnp.int32, (SUBLANES, LANES), 0)

    def head(h, carry):
        row = pl.multiple_of(h * PEER_TOPK, PEER_TOPK)
        for half in range(tm // LANES):
            tok = slice(half * LANES, (half + 1) * LANES)
            stage1 = []
            for c in range(2):
                keys = sk_ref[h, c].astype(BF16)
                q = qy_ref[2 * h + c, tok, :].astype(BF16)
                stage1.append(_extract_top(_dot_nt(keys, q), kio, PEER_TOPK))
            (v0, i0), (v1, i1) = stage1
            v1_lo = jnp.concatenate(v1[:SUBLANES], axis=0)
            groups = [v0[0] + jnp.concatenate(v1, axis=0)]
            for a in range(1, _N_MID + 1):
                groups.append(jnp.where(sub < _PAIR_LIMIT[a], v0[a] + v1_lo, -jnp.inf))
            groups.append(jnp.concatenate(v0[_N_MID + 1:], axis=0) + v1[0])
            top_v, pos = _extract_top(jnp.concatenate(groups, axis=0), pio, PEER_TOPK)
            top_v, pos = jnp.concatenate(top_v, axis=0), jnp.concatenate(pos, axis=0)
            mid = jnp.floor((pos - PEER_TOPK) * (1.0 / SUBLANES))
            last = float(PEER_TOPK + SUBLANES * _N_MID)
            a_sel = jnp.where(pos < PEER_TOPK, 0.0, jnp.where(pos < last, 1.0 + mid, pos - (last - _N_MID - 1)))
            b_sel = jnp.where(pos < PEER_TOPK, pos,
                              jnp.where(pos < last, pos - PEER_TOPK - SUBLANES * mid, 0.0))
            ki = jnp.zeros_like(pos)
            kj = jnp.zeros_like(pos)
            for a in range(PEER_TOPK):
                ki = ki + jnp.where(a_sel == float(a), i0[a], 0.0)
                kj = kj + jnp.where(b_sel == float(a), i1[a], 0.0)
            e = jnp.exp(top_v - top_v[0:1, :])
            ti[pl.ds(row, PEER_TOPK), tok] = ki
            tj[pl.ds(row, PEER_TOPK), tok] = kj
            tg[pl.ds(row, PEER_TOPK), tok] = e / jnp.sum(e, axis=0, keepdims=True)
        return carry

    lax.fori_loop(0, PEER_HEADS, head, 0, unroll=ROUTE_UNROLL)
    i_ref[...] = ti[...].T
    j_ref[...] = tj[...].T
    g_ref[...] = tg[...].T


def peer_route(qy_slab, sub_keys):
    t = qy_slab.shape[1]
    tm = _row_tile(t, ROUTE_TILE)
    slots = PEER_HEADS * PEER_TOPK
    out = jax.ShapeDtypeStruct((t, slots), F32)
    ospec = pl.BlockSpec((tm, slots), lambda i: (i, 0))
    full = pltpu.VMEM((slots, tm), F32)
    return pl.pallas_call(
        _route_kernel,
        out_shape=(out, out, out),
        grid=(t // tm,),
        in_specs=[pl.BlockSpec((2 * PEER_HEADS, tm, HEAD_DIM), lambda i: (0, i, 0)),
                  pl.BlockSpec(sub_keys.shape, lambda i: (0, 0, 0, 0))],
        out_specs=(ospec, ospec, ospec),
        scratch_shapes=[full, full, full],
        compiler_params=_params(("parallel",)),
        name="peer_route",
    )(qy_slab, sub_keys)


def _coef_kernel(i_ref, j_ref, g_ref, c_ref, stage):
    tm = i_ref.shape[0]
    sub = lax.broadcasted_iota(jnp.int32, (N_KEYS, i_ref.shape[1]), 0).astype(F32)

    def group(t0, buf):
        ib = i_ref[pl.ds(t0, COEF_GROUP), :]
        jb = j_ref[pl.ds(t0, COEF_GROUP), :]
        gb = g_ref[pl.ds(t0, COEF_GROUP), :]
        for t in range(COEF_GROUP):
            ptg = jnp.where(sub == ib[t:t + 1, :], gb[t:t + 1, :], 0.0).astype(BF16)
            qt = jnp.where(sub == jb[t:t + 1, :], 1.0, 0.0).astype(BF16)
            buf[t * COEF_PITCH:t * COEF_PITCH + N_KEYS, :] = _dot_nt(ptg, qt)
        for i in range(N_KEYS):
            c_ref[i, pl.ds(t0, COEF_GROUP), :] = buf[pl.ds(i, COEF_GROUP, stride=COEF_PITCH), :].astype(c_ref.dtype)

    def trip(gi, carry):
        for u in range(COEF_UNROLL):
            group(pl.multiple_of((gi * COEF_UNROLL + u) * COEF_GROUP, COEF_GROUP), stage.at[u])
        return carry

    lax.fori_loop(0, tm // (COEF_GROUP * COEF_UNROLL), trip, 0)


def peer_coefficients(ki, kj, gate):
    t, slots = ki.shape
    tm = _row_tile(t, COEF_TILE)
    spec = pl.BlockSpec((tm, slots), lambda i: (i, 0))
    return pl.pallas_call(
        _coef_kernel,
        out_shape=jax.ShapeDtypeStruct((N_KEYS, t, N_KEYS), BF16),
        grid=(t // tm,),
        in_specs=[spec, spec, spec],
        out_specs=pl.BlockSpec((N_KEYS, tm, N_KEYS), lambda i: (0, i, 0)),
        scratch_shapes=[pltpu.VMEM((COEF_UNROLL, COEF_GROUP * COEF_PITCH, N_KEYS), F32)],
        compiler_params=_params(("parallel",)),
        name="peer_coefficients",
    )(ki, kj, gate)


def _gelu(x):
    return 0.5 * x * (1.0 + lax.erf(x * 0.7071067811865476))


def _peer_kernel(x_ref, u_ref, v_ref, c_ref, o_ref):
    @pl.when(pl.program_id(1) == 0)
    def _():
        o_ref[...] = jnp.zeros_like(o_ref)

    act = _gelu(_dot_nt(x_ref[...], u_ref[...]))
    w = jnp.concatenate([(c_ref[s].astype(F32) * act[:, s * N_KEYS:(s + 1) * N_KEYS]).astype(BF16)
                         for s in range(c_ref.shape[0])], axis=1)
    o_ref[...] += jnp.dot(w, v_ref[...], preferred_element_type=F32)


def peer_dense(x, u, v, coef):
    t, d = x.shape
    e = u.shape[0]
    tm, te = _row_tile(t, PEER_ROWS), PEER_EXPERTS
    return pl.pallas_call(
        _peer_kernel,
        out_shape=jax.ShapeDtypeStruct((t, d), F32),
        grid=(t // tm, e // te),
        in_specs=[pl.BlockSpec((tm, d), lambda i, j: (i, 0), pipeline_mode=pl.Buffered(1)),
                  pl.BlockSpec((te, d), lambda i, j: (j, 0)),
                  pl.BlockSpec((te, d), lambda i, j: (j, 0)),
                  pl.BlockSpec((te // N_KEYS, tm, N_KEYS), lambda i, j: (j, i, 0))],
        out_specs=pl.BlockSpec((tm, d), lambda i, j: (i, 0), pipeline_mode=pl.Buffered(1)),
        compiler_params=_params(("parallel", "arbitrary")),
        name="peer_dense",
    )(x, u, v, coef)


def kernel(x_prompt, x_sample, w_in, w_out, norm_mix, norm_ffn, lb_logits, hgrn_norm, attn_norm, attn_sink,
           peer_query, peer_sub_keys, peer_u, peer_v, final_norm):
    depth = w_in.shape[0]
    d_model = x_prompt.shape[-1]
    seq_len = x_prompt.shape[1]
    assert x_sample.shape[1] == seq_len, "prompt and sample sequences are stacked and must share a length"
    n_seq = x_prompt.shape[0] + x_sample.shape[0]
    xp = x_prompt.reshape(-1, d_model)
    xs = x_sample.reshape(-1, d_model)
    t_prompt, t_sample = xp.shape[0], xs.shape[0]
    hw = N_HGRN_HEADS * HEAD_DIM
    q_col = 5 * hw
    k_col = q_col + N_ATTN_HEADS * HEAD_DIM
    v_col = k_col + N_KV_HEADS * HEAD_DIM

    cos, sin_signed = rope_tables(seq_len)
    lbl_t = jnp.transpose(lb_logits.astype(F32), (1, 0, 2))

    h = rmsnorm_rows2(xp, xs, norm_mix[0], BF16)
    x = None
    for l in range(depth):
        p = matmul(h, cast_bf16(w_in, l))
        o_h = hgrn2(p, lbl_t, hgrn_norm[l], l, n_seq, seq_len)
        o_a = window_attention(p, cos, sin_signed, attn_sink[l].astype(F32), attn_norm[l], n_seq, seq_len,
                               q_col, k_col, v_col)
        if l == 0:
            x = matmul2_residual(o_h, o_a, cast_bf16(w_out, l), xp, xs)
        else:
            x = matmul2_residual(o_h, o_a, cast_bf16(w_out, l), x)
        h2 = rmsnorm_rows(x, norm_ffn[l], BF16)
        qy = matmul(h2, cast_bf16(peer_query, l), slab_out=True)
        ki, kj, gate = peer_route(qy, peer_sub_keys[l])
        coef = peer_coefficients(ki, kj, gate)
        po = peer_dense(h2, cast_bf16(peer_u, l), cast_bf16(peer_v, l), coef)
        if l + 1 < depth:
            x, h = add_rmsnorm_rows(x, po, norm_mix[l + 1], BF16)
    y_prompt = add_rmsnorm_final(x, po, final_norm, 0, t_prompt)
    y_sample = add_rmsnorm_final(x, po, final_norm, t_prompt, t_sample)
    return (y_prompt.reshape(x_prompt.shape), y_sample.reshape(x_sample.shape))
```

```python
import functools

import jax
import jax.numpy as jnp
from jax import lax
from jax.experimental import pallas as pl
from jax.experimental.pallas import tpu as pltpu

F32 = jnp.float32
BF16 = jnp.bfloat16

LANES = 128
SUBLANES = 8
HEAD_DIM = LANES
N_HGRN_HEADS = 16
N_ATTN_HEADS = 16
N_KV_HEADS = 4
GQA_REP = N_ATTN_HEADS // N_KV_HEADS
WINDOW = 128
ROPE_THETA = 10000.0
HGRN_SUB = 16
N_KEYS = 128
PEER_HEADS = 8
PEER_TOPK = 16
EPS = 1e-6
NEG_INF = -1e30
LOG2_E = 1.4426950408889634

VMEM_LIMIT = 56 * 1024 * 1024

NORM_ROWS = 256
MM_ROWS = 1024
MM_COLS = 512
CAST_BLOCK = 1024
HGRN_BLOCK = 128
HGRN_GROUP = 4
ATTN_BLOCK = WINDOW
ROUTE_TILE = 256
COEF_TILE = 256
COEF_GROUP = 16
COEF_PITCH = N_KEYS + SUBLANES
COEF_UNROLL = 4
PEER_ROWS = 1024
PEER_EXPERTS = 512


def _params(sem):
    return pltpu.CompilerParams(dimension_semantics=sem, vmem_limit_bytes=VMEM_LIMIT)


def _sigmoid(x):
    return 1.0 / (1.0 + jnp.exp(-x))


def _dot_nt(a, b):
    return lax.dot_general(a, b, (((1,), (1,)), ((), ())), preferred_element_type=F32)


def _row_tile(t, want):
    return want if t % want == 0 else t


def _common_tile(want, *sizes):
    while any(n % want for n in sizes):
        want //= 2
    return want


def _rms(x, g):
    return x * lax.rsqrt(jnp.mean(x * x, axis=-1, keepdims=True) + EPS) * g


def _cast_kernel(w_ref, o_ref):
    o_ref[...] = w_ref[...].astype(o_ref.dtype)


def cast_bf16(w, layer, col_panels=False):
    _, r, c = w.shape
    br, bc = _row_tile(r, CAST_BLOCK), _row_tile(c, CAST_BLOCK)
    if col_panels:
        out_shape = jax.ShapeDtypeStruct((c // bc, r, bc), BF16)
        out_spec = pl.BlockSpec((None, br, bc), lambda i, j: (j, i, 0))
    else:
        out_shape = jax.ShapeDtypeStruct((r, c), BF16)
        out_spec = pl.BlockSpec((br, bc), lambda i, j: (i, j))
    return pl.pallas_call(
        _cast_kernel,
        out_shape=out_shape,
        grid=(r // br, c // bc),
        in_specs=[pl.BlockSpec((None, br, bc), lambda i, j: (layer, i, j))],
        out_specs=out_spec,
        compiler_params=_params(("parallel", "parallel")),
        name="cast_bf16",
    )(w)


def _norm2_kernel(n_a, xa_ref, xb_ref, g_ref, h_ref):
    x = jnp.where(pl.program_id(0) < n_a, xa_ref[...], xb_ref[...])
    h_ref[...] = _rms(x, g_ref[...]).astype(h_ref.dtype)


def _norm_kernel(x_ref, g_ref, h_ref):
    h_ref[...] = _rms(x_ref[...], g_ref[...]).astype(h_ref.dtype)


def _addnorm_kernel(x_ref, y_ref, g_ref, s_ref, h_ref):
    x = x_ref[...] + y_ref[...]
    s_ref[...] = x
    h_ref[...] = _rms(x, g_ref[...]).astype(h_ref.dtype)


def _addnorm_final_kernel(x_ref, y_ref, g_ref, h_ref):
    h_ref[...] = _rms(x_ref[...] + y_ref[...], g_ref[...]).astype(h_ref.dtype)


def _two_source_specs(ta, tm, width):
    n_a = ta // tm
    spec_a = pl.BlockSpec((tm, width), lambda i, *_: (jnp.minimum(i, n_a - 1), 0))
    spec_b = pl.BlockSpec((tm, width), lambda i, *_: (jnp.maximum(i - n_a, 0), 0))
    return n_a, spec_a, spec_b


def rmsnorm_rows2(xa, xb, g, out_dtype):
    ta, d = xa.shape
    t = ta + xb.shape[0]
    tm = _common_tile(NORM_ROWS, ta, t)
    n_a, spec_a, spec_b = _two_source_specs(ta, tm, d)
    return pl.pallas_call(
        functools.partial(_norm2_kernel, n_a),
        out_shape=jax.ShapeDtypeStruct((t, d), out_dtype),
        grid=(t // tm,),
        in_specs=[spec_a, spec_b, pl.BlockSpec((1, d), lambda i: (0, 0))],
        out_specs=pl.BlockSpec((tm, d), lambda i: (i, 0)),
        compiler_params=_params(("parallel",)),
        name="rmsnorm2",
    )(xa, xb, g.reshape(1, d))


def rmsnorm_rows(x, g, out_dtype):
    t, d = x.shape
    tm = _row_tile(t, NORM_ROWS)
    row = pl.BlockSpec((tm, d), lambda i: (i, 0))
    return pl.pallas_call(
        _norm_kernel,
        out_shape=jax.ShapeDtypeStruct((t, d), out_dtype),
        grid=(t // tm,),
        in_specs=[row, pl.BlockSpec((1, d), lambda i: (0, 0))],
        out_specs=row,
        compiler_params=_params(("parallel",)),
        name="rmsnorm",
    )(x, g.reshape(1, d))


def add_rmsnorm_rows(x, y, g, out_dtype):
    t, d = x.shape
    tm = _row_tile(t, NORM_ROWS)
    row = pl.BlockSpec((tm, d), lambda i: (i, 0))
    return pl.pallas_call(
        _addnorm_kernel,
        out_shape=(jax.ShapeDtypeStruct((t, d), F32), jax.ShapeDtypeStruct((t, d), out_dtype)),
        grid=(t // tm,),
        in_specs=[row, row, pl.BlockSpec((1, d), lambda i: (0, 0))],
        out_specs=(row, row),
        compiler_params=_params(("parallel",)),
        name="add_rmsnorm",
    )(x, y, g.reshape(1, d))


def add_rmsnorm_final(x, y, g, row_start, n_rows):
    d = x.shape[1]
    tm = _row_tile(n_rows, NORM_ROWS)
    off = row_start // tm
    src = pl.BlockSpec((tm, d), lambda i: (i + off, 0))
    return pl.pallas_call(
        _addnorm_final_kernel,
        out_shape=jax.ShapeDtypeStruct((n_rows, d), F32),
        grid=(n_rows // tm,),
        in_specs=[src, src, pl.BlockSpec((1, d), lambda i: (0, 0))],
        out_specs=pl.BlockSpec((tm, d), lambda i: (i, 0)),
        compiler_params=_params(("parallel",)),
        name="add_rmsnorm_final",
    )(x, y, g.reshape(1, d))


def _mm_kernel(a_ref, b_ref, o_ref):
    o_ref[...] = jnp.dot(a_ref[...], b_ref[...], preferred_element_type=F32)


def matmul(a, b):
    t, k = a.shape
    n = b.shape[1]
    tm, tn = _row_tile(t, MM_ROWS), MM_COLS
    return pl.pallas_call(
        _mm_kernel,
        out_shape=jax.ShapeDtypeStruct((t, n), F32),
        grid=(t // tm, n // tn),
        in_specs=[pl.BlockSpec((tm, k), lambda i, j: (i, 0)), pl.BlockSpec((k, tn), lambda i, j: (0, j))],
        out_specs=pl.BlockSpec((tm, tn), lambda i, j: (i, j)),
        compiler_params=_params(("parallel", "parallel")),
        name="matmul",
    )(a, b)


def _mm2_res_kernel(a1_ref, a2_ref, b1_ref, b2_ref, r_ref, o_ref):
    acc = jnp.dot(a1_ref[...], b1_ref[...], preferred_element_type=F32)
    acc = acc + jnp.dot(a2_ref[...], b2_ref[...], preferred_element_type=F32)
    o_ref[...] = acc + r_ref[...]


def _mm2_res2_kernel(n_a, a1_ref, a2_ref, b1_ref, b2_ref, ra_ref, rb_ref, o_ref):
    acc = jnp.dot(a1_ref[...], b1_ref[...], preferred_element_type=F32)
    acc = acc + jnp.dot(a2_ref[...], b2_ref[...], preferred_element_type=F32)
    o_ref[...] = acc + jnp.where(pl.program_id(0) < n_a, ra_ref[...], rb_ref[...])


def matmul2_residual(a1, a2, b, res, res_b=None):
    t, kh = a1.shape
    n = b.shape[1]
    tm = _row_tile(t, MM_ROWS) if res_b is None else _common_tile(MM_ROWS, res.shape[0], t)
    tn = MM_COLS
    specs = [
        pl.BlockSpec((tm, kh), lambda i, j: (i, 0)),
        pl.BlockSpec((tm, kh), lambda i, j: (i, 0)),
        pl.BlockSpec((kh, tn), lambda i, j: (0, j)),
        pl.BlockSpec((kh, tn), lambda i, j: (1, j)),
    ]
    if res_b is None:
        body = _mm2_res_kernel
        specs.append(pl.BlockSpec((tm, tn), lambda i, j: (i, j)))
        args = (a1, a2, b, b, res)
    else:
        n_a = res.shape[0] // tm
        body = functools.partial(_mm2_res2_kernel, n_a)
        specs.append(pl.BlockSpec((tm, tn), lambda i, j: (jnp.minimum(i, n_a - 1), j)))
        specs.append(pl.BlockSpec((tm, tn), lambda i, j: (jnp.maximum(i - n_a, 0), j)))
        args = (a1, a2, b, b, res, res_b)
    return pl.pallas_call(
        body,
        out_shape=jax.ShapeDtypeStruct((t, n), F32),
        grid=(t // tm, n // tn),
        in_specs=specs,
        out_specs=pl.BlockSpec((tm, tn), lambda i, j: (i, j)),
        compiler_params=_params(("parallel", "parallel")),
        name="matmul2_residual",
    )(*args)


def _hgrn_masks(cb, width, reverse):
    ti = lax.broadcasted_iota(jnp.int32, (cb, width), 0)
    ji = lax.broadcasted_iota(jnp.int32, (cb, width), 1) % HEAD_DIM
    same = (ti // HGRN_SUB) == (ji // HGRN_SUB)
    masks = [jnp.logical_and(same, (ji >= ti) if reverse else (ji <= ti))]
    hs = HGRN_SUB
    while 2 * hs <= cb:
        grp = (ti // (2 * hs)) == (ji // (2 * hs))
        t_hi = (ti % (2 * hs)) >= hs
        j_hi = (ji % (2 * hs)) >= hs
        if reverse:
            sel = jnp.logical_and(jnp.logical_not(t_hi), j_hi)
        else:
            sel = jnp.logical_and(t_hi, jnp.logical_not(j_hi))
        masks.append(jnp.logical_and(grp, sel))
        hs *= 2
    t1 = lax.broadcasted_iota(jnp.int32, (cb, cb), 0)
    j1 = lax.broadcasted_iota(jnp.int32, (cb, cb), 1)
    tri = ((j1 >= t1) if reverse else (j1 <= t1)).astype(BF16)
    return masks, tri


def _ref_rows(b, group, idx):
    cb, w = b.shape
    g = b.reshape(cb // group, group, w)
    r = jnp.broadcast_to(g[:, idx:idx + 1, :], g.shape)
    return r.reshape(cb, w)


def _cumsum_rows(tri, a):
    w = a.shape[1]
    a1 = a.astype(BF16)
    r1 = a - a1.astype(F32)
    a2 = r1.astype(BF16)
    a3 = (r1 - a2.astype(F32)).astype(BF16)
    out = jnp.dot(tri, jnp.concatenate([a1, a2, a3], axis=1), preferred_element_type=F32)
    return out[:, :w] + out[:, w:2 * w] + out[:, 2 * w:]


def _heads(x):
    return [x[:, h * HEAD_DIM:(h + 1) * HEAD_DIM] for h in range(x.shape[1] // HEAD_DIM)]


def _hgrn_group(qz, fz, v, lb, st_ref, first_head, masks, tri, reverse):
    cb = qz.shape[0]
    q = qz * _sigmoid(qz)
    f = lb + (1.0 - lb) * _sigmoid(fz)
    a = jnp.log(f) * LOG2_E
    k = 1.0 - f
    b = _cumsum_rows(tri, a)
    b_tot = b[0:1, :] if reverse else b[cb - 1:cb, :]

    def scores(eq, ek):
        qs, ks = _heads((q * eq).astype(BF16)), _heads((k * ek).astype(BF16))
        return jnp.concatenate([_dot_nt(qh, kh) for qh, kh in zip(qs, ks)], axis=1)

    r0 = _ref_rows(b, HGRN_SUB, HGRN_SUB // 2 if reverse else HGRN_SUB // 2 - 1)
    att = jnp.where(masks[0], scores(jnp.exp2(b - r0), jnp.exp2(r0 - b)), 0.0)
    hs = HGRN_SUB
    for mask in masks[1:]:
        e = jnp.exp2(-jnp.abs(b - _ref_rows(b, 2 * hs, hs if reverse else hs - 1)))
        att = jnp.where(mask, scores(e, e), att)
        hs *= 2
    att_h = _heads(att.astype(BF16))
    v_h = _heads(v)
    q_dec = _heads((q * jnp.exp2(b)).astype(BF16))
    k_dec = _heads((k * jnp.exp2(b_tot - b)).astype(BF16))
    decay = _heads(jnp.exp2(b_tot))
    outs = []
    for h in range(len(v_h)):
        st = st_ref[first_head + h]
        o = jnp.dot(att_h[h], v_h[h].astype(BF16), preferred_element_type=F32)
        outs.append(o + _dot_nt(q_dec[h], st.astype(BF16)))
        st_ref[first_head + h] = st * decay[h] + jnp.dot(v_h[h].T.astype(BF16), k_dec[h],
                                                         preferred_element_type=F32)
    return jnp.concatenate(outs, axis=1)


def _lower_bound(lbl, layer):
    if layer == 0:
        return jnp.zeros((1, lbl.shape[1]), F32)
    m = jnp.max(lbl, axis=0, keepdims=True)
    e = jnp.exp(lbl - m)
    sm = e / jnp.sum(e, axis=0, keepdims=True)
    return jnp.sum(sm[1:layer + 1, :], axis=0, keepdims=True)


def _hgrn_fwd_kernel(layer, q_ref, f_ref, i_ref, lbl_ref, o_ref, st_ref):
    @pl.when(pl.program_id(1) == 0)
    def _():
        st_ref[...] = jnp.zeros_like(st_ref)

    gw = HGRN_GROUP * HEAD_DIM
    masks, tri = _hgrn_masks(q_ref.shape[0], gw, False)
    lb = _lower_bound(lbl_ref[...], layer)
    for g in range(N_HGRN_HEADS // HGRN_GROUP):
        sl = slice(g * gw, (g + 1) * gw)
        o_ref[:, sl] = _hgrn_group(q_ref[:, sl], f_ref[:, sl], i_ref[:, sl], lb[:, sl], st_ref,
                                   g * HGRN_GROUP, masks, tri, False)


def _hgrn_bwd_kernel(layer, q_ref, f_ref, i_ref, g_ref, of_ref, lbl_ref, nw_ref, o_ref, st_ref):
    @pl.when(pl.program_id(1) == 0)
    def _():
        st_ref[...] = jnp.zeros_like(st_ref)

    gw = HGRN_GROUP * HEAD_DIM
    masks, tri = _hgrn_masks(q_ref.shape[0], gw, True)
    lb = _lower_bound(lbl_ref[...], layer)
    for g in range(N_HGRN_HEADS // HGRN_GROUP):
        sl = slice(g * gw, (g + 1) * gw)
        o = _hgrn_group(q_ref[:, sl], f_ref[:, sl], i_ref[:, sl], lb[:, sl], st_ref,
                        g * HGRN_GROUP, masks, tri, True) + of_ref[:, sl]
        o = jnp.concatenate([oh * lax.rsqrt(jnp.mean(oh * oh, axis=-1, keepdims=True) + EPS)
                             for oh in _heads(o)], axis=1) * nw_ref[:, sl]
        gz = g_ref[:, sl]
        o_ref[:, sl] = (o * (gz * _sigmoid(gz))).astype(o_ref.dtype)


def hgrn2(p, lb_logits_t, norm_w, layer, n_seq, seq_len):
    t = p.shape[0]
    w = N_HGRN_HEADS * HEAD_DIM
    cb = HGRN_BLOCK
    nb = seq_len // cb
    depth = lb_logits_t.shape[1]

    def col(j, reverse):
        if reverse:
            return pl.BlockSpec((cb, w), lambda s, c: (s * nb + nb - 1 - c, j))
        return pl.BlockSpec((cb, w), lambda s, c: (s * nb + c, j))

    def lbl(d):
        return pl.BlockSpec((None, depth, w), lambda s, c: (d, 0, 0))

    scratch = [pltpu.VMEM((N_HGRN_HEADS, HEAD_DIM, HEAD_DIM), F32)]
    o_f = pl.pallas_call(
        functools.partial(_hgrn_fwd_kernel, layer),
        out_shape=jax.ShapeDtypeStruct((t, w), F32),
        grid=(n_seq, nb),
        in_specs=[col(0, False), col(1, False), col(3, False), lbl(0)],
        out_specs=col(0, False),
        scratch_shapes=scratch,
        compiler_params=_params(("parallel", "arbitrary")),
        name="hgrn_fwd",
    )(p, p, p, lb_logits_t)
    return pl.pallas_call(
        functools.partial(_hgrn_bwd_kernel, layer),
        out_shape=jax.ShapeDtypeStruct((t, w), BF16),
        grid=(n_seq, nb),
        in_specs=[col(0, True), col(2, True), col(3, True), col(4, True), col(0, True), lbl(1),
                  pl.BlockSpec((1, w), lambda s, c: (0, 0))],
        out_specs=col(0, True),
        scratch_shapes=scratch,
        compiler_params=_params(("parallel", "arbitrary")),
        name="hgrn_bwd",
    )(p, p, p, p, o_f, lb_logits_t, norm_w.reshape(1, w))


def _rope(x, cos, sin_signed):
    return x * cos + pltpu.roll(x, HEAD_DIM // 2, axis=1) * sin_signed


def _attn_kernel(q_ref, kp_ref, kc_ref, kn_ref, vp_ref, vc_ref, vn_ref,
                 cp_ref, sp_ref, cc_ref, sc_ref, cn_ref, sn_ref, sink_ref, nw_ref, o_ref):
    n = pl.program_id(1)
    nb = pl.num_programs(1)
    tb = q_ref.shape[0]
    rows = GQA_REP * tb
    ri = lax.broadcasted_iota(jnp.int32, (rows, tb), 0) % tb
    ci = lax.broadcasted_iota(jnp.int32, (rows, tb), 1)
    head_of_row = lax.broadcasted_iota(jnp.int32, (rows, 1), 0) // tb
    m_prev = jnp.logical_and(ci >= ri, n > 0)
    m_next = jnp.logical_and(ci <= ri, n < nb - 1)
    scale = HEAD_DIM ** -0.5
    cc, sc = cc_ref[...], sc_ref[...]
    for g in range(N_KV_HEADS):
        sl = slice(g * HEAD_DIM, (g + 1) * HEAD_DIM)
        kcat = jnp.concatenate([
            _rope(kp_ref[:, sl], cp_ref[...], sp_ref[...]),
            _rope(kc_ref[:, sl], cc, sc),
            _rope(kn_ref[:, sl], cn_ref[...], sn_ref[...]),
        ], axis=0).astype(BF16)
        vcat = jnp.concatenate([vp_ref[:, sl], vc_ref[:, sl], vn_ref[:, sl]], axis=0).astype(BF16)
        heads = [g * GQA_REP + r for r in range(GQA_REP)]
        q = jnp.concatenate([_rope(q_ref[:, h * HEAD_DIM:(h + 1) * HEAD_DIM], cc, sc) for h in heads],
                            axis=0).astype(BF16)
        s = _dot_nt(q, kcat) * scale
        s = jnp.concatenate([jnp.where(m_prev, s[:, :tb], NEG_INF), s[:, tb:2 * tb],
                             jnp.where(m_next, s[:, 2 * tb:], NEG_INF)], axis=1)
        sink = jnp.zeros((rows, 1), F32)
        for r, h in enumerate(heads):
            sink = jnp.where(head_of_row == r, sink_ref[h], sink)
        m = jnp.maximum(jnp.max(s, axis=-1, keepdims=True), sink)
        e = jnp.exp(s - m)
        den = jnp.sum(e, axis=-1, keepdims=True) + jnp.exp(sink - m)
        o = jnp.dot(e.astype(BF16), vcat, preferred_element_type=F32) / den
        o = o * lax.rsqrt(jnp.mean(o * o, axis=-1, keepdims=True) + EPS)
        for r, h in enumerate(heads):
            hs = slice(h * HEAD_DIM, (h + 1) * HEAD_DIM)
            o_ref[:, hs] = (o[r * tb:(r + 1) * tb, :] * nw_ref[:, hs]).astype(o_ref.dtype)


def window_attention(p, cos, sin_signed, sink, norm_w, n_seq, seq_len, q_col, k_col, v_col):
    t = p.shape[0]
    tb = ATTN_BLOCK
    nb = seq_len // tb
    qw = N_ATTN_HEADS * HEAD_DIM
    kw = N_KV_HEADS * HEAD_DIM

    def prev(n):
        return jnp.maximum(n - 1, 0)

    def nxt(n):
        return jnp.minimum(n + 1, nb - 1)

    def same(n):
        return n

    def kv(colblk, f):
        return pl.BlockSpec((tb, kw), lambda s, n: (s * nb + f(n), colblk))

    def tab(f):
        return pl.BlockSpec((tb, HEAD_DIM), lambda s, n: (f(n), 0))

    qspec = pl.BlockSpec((tb, qw), lambda s, n: (s * nb + n, q_col // qw))
    kb, vb = k_col // kw, v_col // kw
    return pl.pallas_call(
        _attn_kernel,
        out_shape=jax.ShapeDtypeStruct((t, qw), BF16),
        grid=(n_seq, nb),
        in_specs=[qspec, kv(kb, prev), kv(kb, same), kv(kb, nxt), kv(vb, prev), kv(vb, same), kv(vb, nxt),
                  tab(prev), tab(prev), tab(same), tab(same), tab(nxt), tab(nxt),
                  pl.BlockSpec(memory_space=pltpu.SMEM),
                  pl.BlockSpec((1, qw), lambda s, n: (0, 0))],
        out_specs=pl.BlockSpec((tb, qw), lambda s, n: (s * nb + n, 0)),
        compiler_params=_params(("parallel", "parallel")),
        name="window_attention",
    )(p, p, p, p, p, p, p, cos, sin_signed, cos, sin_signed, cos, sin_signed, sink, norm_w.reshape(1, qw))


def rope_tables(seq_len):
    inv_freq = ROPE_THETA ** (-jnp.arange(0, HEAD_DIM, 2, dtype=F32) / HEAD_DIM)
    ang = jnp.arange(seq_len, dtype=F32)[:, None] * inv_freq[None, :]
    ang = jnp.concatenate([ang, ang], axis=-1)
    sign = jnp.concatenate([-jnp.ones((HEAD_DIM // 2,), F32), jnp.ones((HEAD_DIM // 2,), F32)])
    return jnp.cos(ang), jnp.sin(ang) * sign[None, :]


assert PEER_TOPK == 2 * SUBLANES
_PAIR_LIMIT = [PEER_TOPK // (a + 1) for a in range(PEER_TOPK)]
_N_MID = PEER_TOPK // 2 - 1
_CAND_ROWS = PEER_TOPK + SUBLANES * _N_MID + SUBLANES
assert all(lim <= SUBLANES for lim in _PAIR_LIMIT[1:]) and all(lim == 1 for lim in _PAIR_LIMIT[_N_MID + 1:])


def _extract_top(s, iota, k_out):
    n = s.shape[0]
    vals, idxs = [], []
    for _ in range(k_out):
        m = jnp.max(s, axis=0, keepdims=True)
        idx = jnp.min(jnp.where(s == m, iota, float(n)), axis=0, keepdims=True)
        s = jnp.where(iota == idx, -jnp.inf, s)
        vals.append(m)
        idxs.append(idx)
    return vals, idxs


def _query_route_kernel(x_ref, wq_ref, sk_ref, i_ref, j_ref, g_ref, qy_a, qy_b, ti, tj, tg):
    tm = x_ref.shape[0]
    step = pl.program_id(0)
    slabs = wq_ref.shape[2] // HEAD_DIM
    heads_per_trip = slabs // 2
    kio = lax.broadcasted_iota(jnp.int32, (N_KEYS, LANES), 0).astype(F32)
    pio = lax.broadcasted_iota(jnp.int32, (_CAND_ROWS, LANES), 0).astype(F32)
    sub = lax.broadcasted_iota(jnp.int32, (SUBLANES, LANES), 0)

    @pl.when(step == 0)
    def _():
        qy_b[...] = jnp.zeros_like(qy_b)

    halves = tm // LANES
    assert heads_per_trip * halves == slabs, "one projection slab per (head, token half) unit"

    def unit_scores(h, half, q_read):
        tok = slice(half * LANES, (half + 1) * LANES)
        return [_dot_nt(sk_ref[h, c].astype(BF16), q_read[2 * h + c, tok, :].astype(BF16)) for c in range(2)]

    def unit_route(h, half, scores):
        row = pl.multiple_of(h * PEER_TOPK, PEER_TOPK)
        tok = slice(half * LANES, (half + 1) * LANES)
        (v0, i0), (v1, i1) = [_extract_top(s, kio, PEER_TOPK) for s in scores]
        v1_lo = jnp.concatenate(v1[:SUBLANES], axis=0)
        groups = [v0[0] + jnp.concatenate(v1, axis=0)]
        for a in range(1, _N_MID + 1):
            groups.append(jnp.where(sub < _PAIR_LIMIT[a], v0[a] + v1_lo, -jnp.inf))
        groups.append(jnp.concatenate(v0[_N_MID + 1:], axis=0) + v1[0])
        top_v, pos = _extract_top(jnp.concatenate(groups, axis=0), pio, PEER_TOPK)
        top_v, pos = jnp.concatenate(top_v, axis=0), jnp.concatenate(pos, axis=0)
        mid = jnp.floor((pos - PEER_TOPK) * (1.0 / SUBLANES))
        last = float(PEER_TOPK + SUBLANES * _N_MID)
        a_sel = jnp.where(pos < PEER_TOPK, 0.0, jnp.where(pos < last, 1.0 + mid, pos - (last - _N_MID - 1)))
        b_sel = jnp.where(pos < PEER_TOPK, pos,
                          jnp.where(pos < last, pos - PEER_TOPK - SUBLANES * mid, 0.0))
        ki = jnp.zeros_like(pos)
        kj = jnp.zeros_like(pos)
        for a in range(PEER_TOPK):
            ki = ki + jnp.where(a_sel == float(a), i0[a], 0.0)
            kj = kj + jnp.where(b_sel == float(a), i1[a], 0.0)
        e = jnp.exp(top_v - top_v[0:1, :])
        ti[pl.ds(row, PEER_TOPK), tok] = ki
        tj[pl.ds(row, PEER_TOPK), tok] = kj
        tg[pl.ds(row, PEER_TOPK), tok] = e / jnp.sum(e, axis=0, keepdims=True)

    def run(q_write, q_read):
        def trip(g, carry):
            units = [(g * heads_per_trip + u, half) for u in range(heads_per_trip) for half in range(halves)]
            x = x_ref[...]
            scores = []
            for s, (h, half) in enumerate(units):
                scores.append(unit_scores(h, half, q_read))
                if s % 2 == 1:
                    acc = jnp.dot(x, wq_ref[g, :, (s - 1) * HEAD_DIM:(s + 1) * HEAD_DIM],
                                  preferred_element_type=F32)
                    q_write[g * slabs + s - 1] = acc[:, :HEAD_DIM]
                    q_write[g * slabs + s] = acc[:, HEAD_DIM:]
            for (h, half), sc in zip(units, scores):
                unit_route(h, half, sc)
            return carry

        lax.fori_loop(0, wq_ref.shape[0], trip, 0)

    @pl.when(lax.rem(step, 2) == 0)
    def _():
        run(qy_a, qy_b)

    @pl.when(lax.rem(step, 2) == 1)
    def _():
        run(qy_b, qy_a)

    i_ref[...] = ti[...].T
    j_ref[...] = tj[...].T
    g_ref[...] = tg[...].T


def peer_query_route(x, wq, sub_keys):
    t, d = x.shape
    tm = _row_tile(t, ROUTE_TILE)
    nt = t // tm
    slots = PEER_HEADS * PEER_TOPK
    out = jax.ShapeDtypeStruct((t, slots), F32)
    ospec = pl.BlockSpec((tm, slots), lambda s: (jnp.maximum(s - 1, 0), 0))
    full = pltpu.VMEM((slots, tm), F32)
    queries = pltpu.VMEM((2 * PEER_HEADS, tm, HEAD_DIM), F32)
    return pl.pallas_call(
        _query_route_kernel,
        out_shape=(out, out, out),
        grid=(nt + 1,),
        in_specs=[pl.BlockSpec((tm, d), lambda s: (jnp.minimum(s, nt - 1), 0)),
                  pl.BlockSpec(wq.shape, lambda s: (0, 0, 0), pipeline_mode=pl.Buffered(1)),
                  pl.BlockSpec(sub_keys.shape, lambda s: (0, 0, 0, 0))],
        out_specs=(ospec, ospec, ospec),
        scratch_shapes=[queries, queries, full, full, full],
        compiler_params=_params(("arbitrary",)),
        name="peer_query_route",
    )(x, wq, sub_keys)


def _coef_kernel(i_ref, j_ref, g_ref, c_ref, stage):
    tm = i_ref.shape[0]
    sub = lax.broadcasted_iota(jnp.int32, (N_KEYS, i_ref.shape[1]), 0).astype(F32)

    def group(t0, buf):
        ib = i_ref[pl.ds(t0, COEF_GROUP), :]
        jb = j_ref[pl.ds(t0, COEF_GROUP), :]
        gb = g_ref[pl.ds(t0, COEF_GROUP), :]
        for t in range(COEF_GROUP):
            ptg = jnp.where(sub == ib[t:t + 1, :], gb[t:t + 1, :], 0.0).astype(BF16)
            qt = jnp.where(sub == jb[t:t + 1, :], 1.0, 0.0).astype(BF16)
            buf[t * COEF_PITCH:t * COEF_PITCH + N_KEYS, :] = _dot_nt(ptg, qt)
        for i in range(N_KEYS):
            c_ref[i, pl.ds(t0, COEF_GROUP), :] = buf[pl.ds(i, COEF_GROUP, stride=COEF_PITCH), :].astype(c_ref.dtype)

    def trip(gi, carry):
        for u in range(COEF_UNROLL):
            group(pl.multiple_of((gi * COEF_UNROLL + u) * COEF_GROUP, COEF_GROUP), stage.at[u])
        return carry

    lax.fori_loop(0, tm // (COEF_GROUP * COEF_UNROLL), trip, 0)


def peer_coefficients(ki, kj, gate):
    t, slots = ki.shape
    tm = _row_tile(t, COEF_TILE)
    spec = pl.BlockSpec((tm, slots), lambda i: (i, 0))
    return pl.pallas_call(
        _coef_kernel,
        out_shape=jax.ShapeDtypeStruct((N_KEYS, t, N_KEYS), BF16),
        grid=(t // tm,),
        in_specs=[spec, spec, spec],
        out_specs=pl.BlockSpec((N_KEYS, tm, N_KEYS), lambda i: (0, i, 0)),
        scratch_shapes=[pltpu.VMEM((COEF_UNROLL, COEF_GROUP * COEF_PITCH, N_KEYS), F32)],
        compiler_params=_params(("parallel",)),
        name="peer_coefficients",
    )(ki, kj, gate)


def _gelu(x):
    return 0.5 * x * (1.0 + lax.erf(x * 0.7071067811865476))


def _peer_kernel(x_ref, u_ref, v_ref, c_ref, o_ref):
    @pl.when(pl.program_id(1) == 0)
    def _():
        o_ref[...] = jnp.zeros_like(o_ref)

    act = _gelu(_dot_nt(x_ref[...], u_ref[...]))
    w = jnp.concatenate([(c_ref[s].astype(F32) * act[:, s * N_KEYS:(s + 1) * N_KEYS]).astype(BF16)
                         for s in range(c_ref.shape[0])], axis=1)
    o_ref[...] += jnp.dot(w, v_ref[...], preferred_element_type=F32)


def peer_dense(x, u, v, coef):
    t, d = x.shape
    e = u.shape[0]
    tm, te = _row_tile(t, PEER_ROWS), PEER_EXPERTS
    return pl.pallas_call(
        _peer_kernel,
        out_shape=jax.ShapeDtypeStruct((t, d), F32),
        grid=(t // tm, e // te),
        in_specs=[pl.BlockSpec((tm, d), lambda i, j: (i, 0), pipeline_mode=pl.Buffered(1)),
                  pl.BlockSpec((te, d), lambda i, j: (j, 0)),
                  pl.BlockSpec((te, d), lambda i, j: (j, 0)),
                  pl.BlockSpec((te // N_KEYS, tm, N_KEYS), lambda i, j: (j, i, 0))],
        out_specs=pl.BlockSpec((tm, d), lambda i, j: (i, 0), pipeline_mode=pl.Buffered(1)),
        compiler_params=_params(("parallel", "arbitrary")),
        name="peer_dense",
    )(x, u, v, coef)


def kernel(x_prompt, x_sample, w_in, w_out, norm_mix, norm_ffn, lb_logits, hgrn_norm, attn_norm, attn_sink,
           peer_query, peer_sub_keys, peer_u, peer_v, final_norm):
    depth = w_in.shape[0]
    d_model = x_prompt.shape[-1]
    seq_len = x_prompt.shape[1]
    assert x_sample.shape[1] == seq_len, "prompt and sample sequences are stacked and must share a length"
    n_seq = x_prompt.shape[0] + x_sample.shape[0]
    xp = x_prompt.reshape(-1, d_model)
    xs = x_sample.reshape(-1, d_model)
    t_prompt, t_sample = xp.shape[0], xs.shape[0]
    hw = N_HGRN_HEADS * HEAD_DIM
    q_col = 5 * hw
    k_col = q_col + N_ATTN_HEADS * HEAD_DIM
    v_col = k_col + N_KV_HEADS * HEAD_DIM

    cos, sin_signed = rope_tables(seq_len)
    lbl_t = jnp.transpose(lb_logits.astype(F32), (1, 0, 2))

    h = rmsnorm_rows2(xp, xs, norm_mix[0], BF16)
    x = None
    for l in range(depth):
        p = matmul(h, cast_bf16(w_in, l))
        o_h = hgrn2(p, lbl_t, hgrn_norm[l], l, n_seq, seq_len)
        o_a = window_attention(p, cos, sin_signed, attn_sink[l].astype(F32), attn_norm[l], n_seq, seq_len,
                               q_col, k_col, v_col)
        if l == 0:
            x = matmul2_residual(o_h, o_a, cast_bf16(w_out, l), xp, xs)
        else:
            x = matmul2_residual(o_h, o_a, cast_bf16(w_out, l), x)
        h2 = rmsnorm_rows(x, norm_ffn[l], BF16)
        ki, kj, gate = peer_query_route(h2, cast_bf16(peer_query, l, col_panels=True), peer_sub_keys[l])
        coef = peer_coefficients(ki, kj, gate)
        po = peer_dense(h2, cast_bf16(peer_u, l), cast_bf16(peer_v, l), coef)
        if l + 1 < depth:
            x, h = add_rmsnorm_rows(x, po, norm_mix[l + 1], BF16)
    y_prompt = add_rmsnorm_final(x, po, final_norm, 0, t_prompt)
    y_sample = add_rmsnorm_final(x, po, final_norm, t_prompt, t_sample)
    return (y_prompt.reshape(x_prompt.shape), y_sample.reshape(x_sample.shape))
```

```python
import functools

import jax
import jax.numpy as jnp
from jax import lax
from jax.experimental import pallas as pl
from jax.experimental.pallas import tpu as pltpu

F32 = jnp.float32
BF16 = jnp.bfloat16

LANES = 128
SUBLANES = 8
BF16_ROWS = 2 * SUBLANES
HEAD_DIM = LANES
N_HGRN_HEADS = 16
N_ATTN_HEADS = 16
N_KV_HEADS = 4
GQA_REP = N_ATTN_HEADS // N_KV_HEADS
WINDOW = 128
ROPE_THETA = 10000.0
HGRN_SUB = 16
N_KEYS = 128
PEER_HEADS = 8
PEER_TOPK = 16
EPS = 1e-6
NEG_INF = -1e30
LOG2_E = 1.4426950408889634

VMEM_LIMIT = 56 * 1024 * 1024

NORM_ROWS = 256
MM_ROWS = 1024
MM_COLS = 512
CAST_BLOCK = 1024
HGRN_BLOCK = 128
HGRN_GROUP = 8
ATTN_BLOCK = WINDOW
ROUTE_TILE = 256
COEF_TILE = 256
COEF_GROUP = 16
COEF_PITCH = N_KEYS + SUBLANES
COEF_UNROLL = 8
PEER_ROWS = 1024
PEER_EXPERTS = 512


def _params(sem):
    return pltpu.CompilerParams(dimension_semantics=sem, vmem_limit_bytes=VMEM_LIMIT)


def _sigmoid(x):
    return 1.0 / (1.0 + jnp.exp(-x))


def _dot_nt(a, b):
    return lax.dot_general(a, b, (((1,), (1,)), ((), ())), preferred_element_type=F32)


def _row_tile(t, want):
    return want if t % want == 0 else t


def _common_tile(want, *sizes):
    while any(n % want for n in sizes):
        want //= 2
    return want


def _rms(x, g):
    return x * lax.rsqrt(jnp.mean(x * x, axis=-1, keepdims=True) + EPS) * g


def _cast_kernel(w_ref, o_ref):
    o_ref[...] = w_ref[...].astype(o_ref.dtype)


def cast_bf16(w, layer, col_panels=False):
    _, r, c = w.shape
    br, bc = _row_tile(r, CAST_BLOCK), _row_tile(c, CAST_BLOCK)
    if col_panels:
        out_shape = jax.ShapeDtypeStruct((c // bc, r, bc), BF16)
        out_spec = pl.BlockSpec((None, br, bc), lambda i, j: (j, i, 0))
    else:
        out_shape = jax.ShapeDtypeStruct((r, c), BF16)
        out_spec = pl.BlockSpec((br, bc), lambda i, j: (i, j))
    return pl.pallas_call(
        _cast_kernel,
        out_shape=out_shape,
        grid=(r // br, c // bc),
        in_specs=[pl.BlockSpec((None, br, bc), lambda i, j: (layer, i, j))],
        out_specs=out_spec,
        compiler_params=_params(("parallel", "parallel")),
        name="cast_bf16",
    )(w)


def _norm2_kernel(n_a, xa_ref, xb_ref, g_ref, h_ref):
    x = jnp.where(pl.program_id(0) < n_a, xa_ref[...], xb_ref[...])
    h_ref[...] = _rms(x, g_ref[...]).astype(h_ref.dtype)


def _norm_kernel(x_ref, g_ref, h_ref):
    h_ref[...] = _rms(x_ref[...], g_ref[...]).astype(h_ref.dtype)


def _addnorm_kernel(x_ref, y_ref, g_ref, s_ref, h_ref):
    x = x_ref[...] + y_ref[...]
    s_ref[...] = x
    h_ref[...] = _rms(x, g_ref[...]).astype(h_ref.dtype)


def _addnorm_final_kernel(x_ref, y_ref, g_ref, h_ref):
    h_ref[...] = _rms(x_ref[...] + y_ref[...], g_ref[...]).astype(h_ref.dtype)


def _two_source_specs(ta, tm, width):
    n_a = ta // tm
    spec_a = pl.BlockSpec((tm, width), lambda i, *_: (jnp.minimum(i, n_a - 1), 0))
    spec_b = pl.BlockSpec((tm, width), lambda i, *_: (jnp.maximum(i - n_a, 0), 0))
    return n_a, spec_a, spec_b


def rmsnorm_rows2(xa, xb, g, out_dtype):
    ta, d = xa.shape
    t = ta + xb.shape[0]
    tm = _common_tile(NORM_ROWS, ta, t)
    n_a, spec_a, spec_b = _two_source_specs(ta, tm, d)
    return pl.pallas_call(
        functools.partial(_norm2_kernel, n_a),
        out_shape=jax.ShapeDtypeStruct((t, d), out_dtype),
        grid=(t // tm,),
        in_specs=[spec_a, spec_b, pl.BlockSpec((1, d), lambda i: (0, 0))],
        out_specs=pl.BlockSpec((tm, d), lambda i: (i, 0)),
        compiler_params=_params(("parallel",)),
        name="rmsnorm2",
    )(xa, xb, g.reshape(1, d))


def rmsnorm_rows(x, g, out_dtype):
    t, d = x.shape
    tm = _row_tile(t, NORM_ROWS)
    row = pl.BlockSpec((tm, d), lambda i: (i, 0))
    return pl.pallas_call(
        _norm_kernel,
        out_shape=jax.ShapeDtypeStruct((t, d), out_dtype),
        grid=(t // tm,),
        in_specs=[row, pl.BlockSpec((1, d), lambda i: (0, 0))],
        out_specs=row,
        compiler_params=_params(("parallel",)),
        name="rmsnorm",
    )(x, g.reshape(1, d))


def add_rmsnorm_rows(x, y, g, out_dtype):
    t, d = x.shape
    tm = _row_tile(t, NORM_ROWS)
    row = pl.BlockSpec((tm, d), lambda i: (i, 0))
    return pl.pallas_call(
        _addnorm_kernel,
        out_shape=(jax.ShapeDtypeStruct((t, d), F32), jax.ShapeDtypeStruct((t, d), out_dtype)),
        grid=(t // tm,),
        in_specs=[row, row, pl.BlockSpec((1, d), lambda i: (0, 0))],
        out_specs=(row, row),
        compiler_params=_params(("parallel",)),
        name="add_rmsnorm",
    )(x, y, g.reshape(1, d))


def add_rmsnorm_final(x, y, g, row_start, n_rows):
    d = x.shape[1]
    tm = _row_tile(n_rows, NORM_ROWS)
    off = row_start // tm
    src = pl.BlockSpec((tm, d), lambda i: (i + off, 0))
    return pl.pallas_call(
        _addnorm_final_kernel,
        out_shape=jax.ShapeDtypeStruct((n_rows, d), F32),
        grid=(n_rows // tm,),
        in_specs=[src, src, pl.BlockSpec((1, d), lambda i: (0, 0))],
        out_specs=pl.BlockSpec((tm, d), lambda i: (i, 0)),
        compiler_params=_params(("parallel",)),
        name="add_rmsnorm_final",
    )(x, y, g.reshape(1, d))


def _mm_cast_kernel(n_side, a_ref, b_ref, *refs):
    o_ref = refs[n_side]
    o_ref[...] = jnp.dot(a_ref[...], b_ref[...], preferred_element_type=F32)
    for w_ref, c_ref in zip(refs[:n_side], refs[n_side + 1:]):
        c_ref[...] = w_ref[...].astype(c_ref.dtype)


def matmul(a, b, side_casts=(), layer=0):
    t, k = a.shape
    n = b.shape[1]
    tm, tn = _row_tile(t, MM_ROWS), MM_COLS
    gi, gj = t // tm, n // tn
    in_specs = [pl.BlockSpec((tm, k), lambda i, j: (i, 0)), pl.BlockSpec((k, tn), lambda i, j: (0, j))]
    out_shape = [jax.ShapeDtypeStruct((t, n), F32)]
    out_specs = [pl.BlockSpec((tm, tn), lambda i, j: (i, j))]
    for w in side_casts:
        _, r, c = w.shape
        rows = BF16_ROWS
        while rows * gi * gj < r:
            rows *= 2
        assert r % rows == 0
        last = r // rows - 1
        in_specs.append(pl.BlockSpec((None, rows, c),
                                     lambda i, j, last=last: (layer, jnp.minimum(i * gj + j, last), 0)))
        out_shape.append(jax.ShapeDtypeStruct((r, c), BF16))
        out_specs.append(pl.BlockSpec((rows, c), lambda i, j, last=last: (jnp.minimum(i * gj + j, last), 0)))
    res = pl.pallas_call(
        functools.partial(_mm_cast_kernel, len(side_casts)),
        out_shape=tuple(out_shape),
        grid=(gi, gj),
        in_specs=in_specs,
        out_specs=tuple(out_specs),
        compiler_params=_params(("arbitrary", "arbitrary")),
        name="matmul",
    )(a, b, *side_casts)
    return res if side_casts else res[0]


def _mm2_res_kernel(a1_ref, a2_ref, b1_ref, b2_ref, r_ref, o_ref):
    acc = jnp.dot(a1_ref[...], b1_ref[...], preferred_element_type=F32)
    acc = acc + jnp.dot(a2_ref[...], b2_ref[...], preferred_element_type=F32)
    o_ref[...] = acc + r_ref[...]


def _mm2_res2_kernel(n_a, a1_ref, a2_ref, b1_ref, b2_ref, ra_ref, rb_ref, o_ref):
    acc = jnp.dot(a1_ref[...], b1_ref[...], preferred_element_type=F32)
    acc = acc + jnp.dot(a2_ref[...], b2_ref[...], preferred_element_type=F32)
    o_ref[...] = acc + jnp.where(pl.program_id(0) < n_a, ra_ref[...], rb_ref[...])


def matmul2_residual(a1, a2, b, res, res_b=None):
    t, kh = a1.shape
    n = b.shape[1]
    tm = _row_tile(t, MM_ROWS) if res_b is None else _common_tile(MM_ROWS, res.shape[0], t)
    tn = MM_COLS
    specs = [
        pl.BlockSpec((tm, kh), lambda i, j: (i, 0)),
        pl.BlockSpec((tm, kh), lambda i, j: (i, 0)),
        pl.BlockSpec((kh, tn), lambda i, j: (0, j)),
        pl.BlockSpec((kh, tn), lambda i, j: (1, j)),
    ]
    if res_b is None:
        body = _mm2_res_kernel
        specs.append(pl.BlockSpec((tm, tn), lambda i, j: (i, j)))
        args = (a1, a2, b, b, res)
    else:
        n_a = res.shape[0] // tm
        body = functools.partial(_mm2_res2_kernel, n_a)
        specs.append(pl.BlockSpec((tm, tn), lambda i, j: (jnp.minimum(i, n_a - 1), j)))
        specs.append(pl.BlockSpec((tm, tn), lambda i, j: (jnp.maximum(i - n_a, 0), j)))
        args = (a1, a2, b, b, res, res_b)
    return pl.pallas_call(
        body,
        out_shape=jax.ShapeDtypeStruct((t, n), F32),
        grid=(t // tm, n // tn),
        in_specs=specs,
        out_specs=pl.BlockSpec((tm, tn), lambda i, j: (i, j)),
        compiler_params=_params(("parallel", "parallel")),
        name="matmul2_residual",
    )(*args)


def _hgrn_masks(cb, width, reverse):
    ti = lax.broadcasted_iota(jnp.int32, (cb, width), 0)
    ji = lax.broadcasted_iota(jnp.int32, (cb, width), 1) % HEAD_DIM
    same = (ti // HGRN_SUB) == (ji // HGRN_SUB)
    masks = [jnp.logical_and(same, (ji >= ti) if reverse else (ji <= ti))]
    hs = HGRN_SUB
    while 2 * hs <= cb:
        grp = (ti // (2 * hs)) == (ji // (2 * hs))
        t_hi = (ti % (2 * hs)) >= hs
        j_hi = (ji % (2 * hs)) >= hs
        if reverse:
            sel = jnp.logical_and(jnp.logical_not(t_hi), j_hi)
        else:
            sel = jnp.logical_and(t_hi, jnp.logical_not(j_hi))
        masks.append(jnp.logical_and(grp, sel))
        hs *= 2
    t1 = lax.broadcasted_iota(jnp.int32, (cb, cb), 0)
    j1 = lax.broadcasted_iota(jnp.int32, (cb, cb), 1)
    tri = ((j1 >= t1) if reverse else (j1 <= t1)).astype(BF16)
    return masks, tri


def _ref_rows(b, group, idx):
    cb, w = b.shape
    g = b.reshape(cb // group, group, w)
    r = jnp.broadcast_to(g[:, idx:idx + 1, :], g.shape)
    return r.reshape(cb, w)


def _cumsum_rows(tri, a):
    w = a.shape[1]
    a1 = a.astype(BF16)
    r1 = a - a1.astype(F32)
    a2 = r1.astype(BF16)
    a3 = (r1 - a2.astype(F32)).astype(BF16)
    out = jnp.dot(tri, jnp.concatenate([a1, a2, a3], axis=1), preferred_element_type=F32)
    return out[:, :w] + out[:, w:2 * w] + out[:, 2 * w:]


def _heads(x):
    return [x[:, h * HEAD_DIM:(h + 1) * HEAD_DIM] for h in range(x.shape[1] // HEAD_DIM)]


def _hgrn_group(qz, fz, v, lb, st_ref, first_head, masks, tri, reverse):
    cb = qz.shape[0]
    q = qz * _sigmoid(qz)
    f = lb + (1.0 - lb) * _sigmoid(fz)
    a = jnp.log(f) * LOG2_E
    k = 1.0 - f
    b = _cumsum_rows(tri, a)
    b_tot = b[0:1, :] if reverse else b[cb - 1:cb, :]

    def scores(eq, ek):
        qs, ks = _heads((q * eq).astype(BF16)), _heads((k * ek).astype(BF16))
        return jnp.concatenate([_dot_nt(qh, kh) for qh, kh in zip(qs, ks)], axis=1)

    r0 = _ref_rows(b, HGRN_SUB, HGRN_SUB // 2 if reverse else HGRN_SUB // 2 - 1)
    att = jnp.where(masks[0], scores(jnp.exp2(b - r0), jnp.exp2(r0 - b)), 0.0)
    hs = HGRN_SUB
    for mask in masks[1:]:
        e = jnp.exp2(-jnp.abs(b - _ref_rows(b, 2 * hs, hs if reverse else hs - 1)))
        att = jnp.where(mask, scores(e, e), att)
        hs *= 2
    att_h = _heads(att.astype(BF16))
    v_h = _heads(v)
    q_dec = _heads((q * jnp.exp2(b)).astype(BF16))
    k_dec = _heads((k * jnp.exp2(b_tot - b)).astype(BF16))
    decay = _heads(jnp.exp2(b_tot))
    outs = []
    for h in range(len(v_h)):
        st = st_ref[first_head + h]
        o = jnp.dot(att_h[h], v_h[h].astype(BF16), preferred_element_type=F32)
        outs.append(o + _dot_nt(q_dec[h], st.astype(BF16)))
        st_ref[first_head + h] = st * decay[h] + jnp.dot(v_h[h].T.astype(BF16), k_dec[h],
                                                         preferred_element_type=F32)
    return jnp.concatenate(outs, axis=1)


def _lower_bound(lbl, layer):
    if layer == 0:
        return jnp.zeros((1, lbl.shape[1]), F32)
    m = jnp.max(lbl, axis=0, keepdims=True)
    e = jnp.exp(lbl - m)
    sm = e / jnp.sum(e, axis=0, keepdims=True)
    return jnp.sum(sm[1:layer + 1, :], axis=0, keepdims=True)


def _hgrn_fwd_kernel(layer, q_ref, f_ref, i_ref, lbl_ref, o_ref, st_ref):
    @pl.when(pl.program_id(1) == 0)
    def _():
        st_ref[...] = jnp.zeros_like(st_ref)

    gw = HGRN_GROUP * HEAD_DIM
    masks, tri = _hgrn_masks(q_ref.shape[0], gw, False)
    lb = _lower_bound(lbl_ref[...], layer)
    for g in range(N_HGRN_HEADS // HGRN_GROUP):
        sl = slice(g * gw, (g + 1) * gw)
        o_ref[:, sl] = _hgrn_group(q_ref[:, sl], f_ref[:, sl], i_ref[:, sl], lb[:, sl], st_ref,
                                   g * HGRN_GROUP, masks, tri, False)


def _hgrn_bwd_kernel(layer, q_ref, f_ref, i_ref, g_ref, of_ref, lbl_ref, nw_ref, o_ref, st_ref):
    @pl.when(pl.program_id(1) == 0)
    def _():
        st_ref[...] = jnp.zeros_like(st_ref)

    gw = HGRN_GROUP * HEAD_DIM
    masks, tri = _hgrn_masks(q_ref.shape[0], gw, True)
    lb = _lower_bound(lbl_ref[...], layer)
    for g in range(N_HGRN_HEADS // HGRN_GROUP):
        sl = slice(g * gw, (g + 1) * gw)
        o = _hgrn_group(q_ref[:, sl], f_ref[:, sl], i_ref[:, sl], lb[:, sl], st_ref,
                        g * HGRN_GROUP, masks, tri, True) + of_ref[:, sl]
        o = jnp.concatenate([oh * lax.rsqrt(jnp.mean(oh * oh, axis=-1, keepdims=True) + EPS)
                             for oh in _heads(o)], axis=1) * nw_ref[:, sl]
        gz = g_ref[:, sl]
        o_ref[:, sl] = (o * (gz * _sigmoid(gz))).astype(o_ref.dtype)


def hgrn2(p, lb_logits_t, norm_w, layer, n_seq, seq_len):
    t = p.shape[0]
    w = N_HGRN_HEADS * HEAD_DIM
    cb = HGRN_BLOCK
    nb = seq_len // cb
    depth = lb_logits_t.shape[1]

    def col(j, reverse):
        if reverse:
            return pl.BlockSpec((cb, w), lambda s, c: (s * nb + nb - 1 - c, j))
        return pl.BlockSpec((cb, w), lambda s, c: (s * nb + c, j))

    def lbl(d):
        return pl.BlockSpec((None, depth, w), lambda s, c: (d, 0, 0))

    scratch = [pltpu.VMEM((N_HGRN_HEADS, HEAD_DIM, HEAD_DIM), F32)]
    o_f = pl.pallas_call(
        functools.partial(_hgrn_fwd_kernel, layer),
        out_shape=jax.ShapeDtypeStruct((t, w), F32),
        grid=(n_seq, nb),
        in_specs=[col(0, False), col(1, False), col(3, False), lbl(0)],
        out_specs=col(0, False),
        scratch_shapes=scratch,
        compiler_params=_params(("parallel", "arbitrary")),
        name="hgrn_fwd",
    )(p, p, p, lb_logits_t)
    return pl.pallas_call(
        functools.partial(_hgrn_bwd_kernel, layer),
        out_shape=jax.ShapeDtypeStruct((t, w), BF16),
        grid=(n_seq, nb),
        in_specs=[col(0, True), col(2, True), col(3, True), col(4, True), col(0, True), lbl(1),
                  pl.BlockSpec((1, w), lambda s, c: (0, 0))],
        out_specs=col(0, True),
        scratch_shapes=scratch,
        compiler_params=_params(("parallel", "arbitrary")),
        name="hgrn_bwd",
    )(p, p, p, p, o_f, lb_logits_t, norm_w.reshape(1, w))


def _rope(x, cos, sin_signed):
    return x * cos + pltpu.roll(x, HEAD_DIM // 2, axis=1) * sin_signed


def _attn_kernel(q_ref, kp_ref, kc_ref, kn_ref, vp_ref, vc_ref, vn_ref,
                 cp_ref, sp_ref, cc_ref, sc_ref, cn_ref, sn_ref, sink_ref, nw_ref, o_ref):
    n = pl.program_id(1)
    nb = pl.num_programs(1)
    tb = q_ref.shape[0]
    rows = GQA_REP * tb
    ri = lax.broadcasted_iota(jnp.int32, (rows, tb), 0) % tb
    ci = lax.broadcasted_iota(jnp.int32, (rows, tb), 1)
    head_of_row = lax.broadcasted_iota(jnp.int32, (rows, 1), 0) // tb
    m_prev = jnp.logical_and(ci >= ri, n > 0)
    m_next = jnp.logical_and(ci <= ri, n < nb - 1)
    scale = HEAD_DIM ** -0.5
    cc, sc = cc_ref[...], sc_ref[...]
    for g in range(N_KV_HEADS):
        sl = slice(g * HEAD_DIM, (g + 1) * HEAD_DIM)
        kcat = jnp.concatenate([
            _rope(kp_ref[:, sl], cp_ref[...], sp_ref[...]),
            _rope(kc_ref[:, sl], cc, sc),
            _rope(kn_ref[:, sl], cn_ref[...], sn_ref[...]),
        ], axis=0).astype(BF16)
        vcat = jnp.concatenate([vp_ref[:, sl], vc_ref[:, sl], vn_ref[:, sl]], axis=0).astype(BF16)
        heads = [g * GQA_REP + r for r in range(GQA_REP)]
        q = jnp.concatenate([_rope(q_ref[:, h * HEAD_DIM:(h + 1) * HEAD_DIM], cc, sc) for h in heads],
                            axis=0).astype(BF16)
        s = _dot_nt(q, kcat) * scale
        s = jnp.concatenate([jnp.where(m_prev, s[:, :tb], NEG_INF), s[:, tb:2 * tb],
                             jnp.where(m_next, s[:, 2 * tb:], NEG_INF)], axis=1)
        sink = jnp.zeros((rows, 1), F32)
        for r, h in enumerate(heads):
            sink = jnp.where(head_of_row == r, sink_ref[h], sink)
        m = jnp.maximum(jnp.max(s, axis=-1, keepdims=True), sink)
        e = jnp.exp(s - m)
        den = jnp.sum(e, axis=-1, keepdims=True) + jnp.exp(sink - m)
        o = jnp.dot(e.astype(BF16), vcat, preferred_element_type=F32) / den
        o = o * lax.rsqrt(jnp.mean(o * o, axis=-1, keepdims=True) + EPS)
        for r, h in enumerate(heads):
            hs = slice(h * HEAD_DIM, (h + 1) * HEAD_DIM)
            o_ref[:, hs] = (o[r * tb:(r + 1) * tb, :] * nw_ref[:, hs]).astype(o_ref.dtype)


def window_attention(p, cos, sin_signed, sink, norm_w, n_seq, seq_len, q_col, k_col, v_col):
    t = p.shape[0]
    tb = ATTN_BLOCK
    nb = seq_len // tb
    qw = N_ATTN_HEADS * HEAD_DIM
    kw = N_KV_HEADS * HEAD_DIM

    def prev(n):
        return jnp.maximum(n - 1, 0)

    def nxt(n):
        return jnp.minimum(n + 1, nb - 1)

    def same(n):
        return n

    def kv(colblk, f):
        return pl.BlockSpec((tb, kw), lambda s, n: (s * nb + f(n), colblk))

    def tab(f):
        return pl.BlockSpec((tb, HEAD_DIM), lambda s, n: (f(n), 0))

    qspec = pl.BlockSpec((tb, qw), lambda s, n: (s * nb + n, q_col // qw))
    kb, vb = k_col // kw, v_col // kw
    return pl.pallas_call(
        _attn_kernel,
        out_shape=jax.ShapeDtypeStruct((t, qw), BF16),
        grid=(n_seq, nb),
        in_specs=[qspec, kv(kb, prev), kv(kb, same), kv(kb, nxt), kv(vb, prev), kv(vb, same), kv(vb, nxt),
                  tab(prev), tab(prev), tab(same), tab(same), tab(nxt), tab(nxt),
                  pl.BlockSpec(memory_space=pltpu.SMEM),
                  pl.BlockSpec((1, qw), lambda s, n: (0, 0))],
        out_specs=pl.BlockSpec((tb, qw), lambda s, n: (s * nb + n, 0)),
        compiler_params=_params(("parallel", "parallel")),
        name="window_attention",
    )(p, p, p, p, p, p, p, cos, sin_signed, cos, sin_signed, cos, sin_signed, sink, norm_w.reshape(1, qw))


def rope_tables(seq_len):
    inv_freq = ROPE_THETA ** (-jnp.arange(0, HEAD_DIM, 2, dtype=F32) / HEAD_DIM)
    ang = jnp.arange(seq_len, dtype=F32)[:, None] * inv_freq[None, :]
    ang = jnp.concatenate([ang, ang], axis=-1)
    sign = jnp.concatenate([-jnp.ones((HEAD_DIM // 2,), F32), jnp.ones((HEAD_DIM // 2,), F32)])
    return jnp.cos(ang), jnp.sin(ang) * sign[None, :]


assert PEER_TOPK == 2 * SUBLANES
_PAIR_LIMIT = [PEER_TOPK // (a + 1) for a in range(PEER_TOPK)]
_N_MID = PEER_TOPK // 2 - 1
_CAND_ROWS = PEER_TOPK + SUBLANES * _N_MID + SUBLANES
assert all(lim <= SUBLANES for lim in _PAIR_LIMIT[1:]) and all(lim == 1 for lim in _PAIR_LIMIT[_N_MID + 1:])


def _extract_top(s, iota, k_out):
    n = s.shape[0]
    vals, idxs = [], []
    for _ in range(k_out):
        m = jnp.max(s, axis=0, keepdims=True)
        idx = jnp.min(jnp.where(s == m, iota, float(n)), axis=0, keepdims=True)
        s = jnp.where(iota == idx, -jnp.inf, s)
        vals.append(m)
        idxs.append(idx)
    return vals, idxs


def _query_route_kernel(x_ref, wq_ref, sk_ref, i_ref, j_ref, g_ref, qy_a, qy_b, ti, tj, tg):
    tm = x_ref.shape[0]
    step = pl.program_id(0)
    slabs = wq_ref.shape[2] // HEAD_DIM
    heads_per_trip = slabs // 2
    kio = lax.broadcasted_iota(jnp.int32, (N_KEYS, LANES), 0).astype(F32)
    pio = lax.broadcasted_iota(jnp.int32, (_CAND_ROWS, LANES), 0).astype(F32)
    sub = lax.broadcasted_iota(jnp.int32, (SUBLANES, LANES), 0)

    @pl.when(step == 0)
    def _():
        qy_b[...] = jnp.zeros_like(qy_b)

    halves = tm // LANES
    assert heads_per_trip * halves == slabs, "one projection slab per (head, token half) unit"

    def unit_scores(h, half, q_read):
        tok = slice(half * LANES, (half + 1) * LANES)
        return [_dot_nt(sk_ref[h, c].astype(BF16), q_read[2 * h + c, tok, :].astype(BF16)) for c in range(2)]

    def unit_route(h, half, scores):
        row = pl.multiple_of(h * PEER_TOPK, PEER_TOPK)
        tok = slice(half * LANES, (half + 1) * LANES)
        (v0, i0), (v1, i1) = [_extract_top(s, kio, PEER_TOPK) for s in scores]
        v1_lo = jnp.concatenate(v1[:SUBLANES], axis=0)
        groups = [v0[0] + jnp.concatenate(v1, axis=0)]
        for a in range(1, _N_MID + 1):
            groups.append(jnp.where(sub < _PAIR_LIMIT[a], v0[a] + v1_lo, -jnp.inf))
        groups.append(jnp.concatenate(v0[_N_MID + 1:], axis=0) + v1[0])
        top_v, pos = _extract_top(jnp.concatenate(groups, axis=0), pio, PEER_TOPK)
        top_v, pos = jnp.concatenate(top_v, axis=0), jnp.concatenate(pos, axis=0)
        mid = jnp.floor((pos - PEER_TOPK) * (1.0 / SUBLANES))
        last = float(PEER_TOPK + SUBLANES * _N_MID)
        a_sel = jnp.where(pos < PEER_TOPK, 0.0, jnp.where(pos < last, 1.0 + mid, pos - (last - _N_MID - 1)))
        b_sel = jnp.where(pos < PEER_TOPK, pos,
                          jnp.where(pos < last, pos - PEER_TOPK - SUBLANES * mid, 0.0))
        ki = jnp.zeros_like(pos)
        kj = jnp.zeros_like(pos)
        for a in range(PEER_TOPK):
            ki = ki + jnp.where(a_sel == float(a), i0[a], 0.0)
            kj = kj + jnp.where(b_sel == float(a), i1[a], 0.0)
        e = jnp.exp(top_v - top_v[0:1, :])
        ti[pl.ds(row, PEER_TOPK), tok] = ki
        tj[pl.ds(row, PEER_TOPK), tok] = kj
        tg[pl.ds(row, PEER_TOPK), tok] = e / jnp.sum(e, axis=0, keepdims=True)

    def run(q_write, q_read):
        def trip(g, carry):
            units = [(g * heads_per_trip + u, half) for u in range(heads_per_trip) for half in range(halves)]
            x = x_ref[...]
            scores = []
            for s, (h, half) in enumerate(units):
                scores.append(unit_scores(h, half, q_read))
                if s % 2 == 1:
                    acc = jnp.dot(x, wq_ref[g, :, (s - 1) * HEAD_DIM:(s + 1) * HEAD_DIM],
                                  preferred_element_type=F32)
                    q_write[g * slabs + s - 1] = acc[:, :HEAD_DIM]
                    q_write[g * slabs + s] = acc[:, HEAD_DIM:]
            for (h, half), sc in zip(units, scores):
                unit_route(h, half, sc)
            return carry

        lax.fori_loop(0, wq_ref.shape[0], trip, 0)

    @pl.when(lax.rem(step, 2) == 0)
    def _():
        run(qy_a, qy_b)

    @pl.when(lax.rem(step, 2) == 1)
    def _():
        run(qy_b, qy_a)

    i_ref[...] = ti[...].T
    j_ref[...] = tj[...].T
    g_ref[...] = tg[...].T


def peer_query_route(x, wq, sub_keys):
    t, d = x.shape
    tm = _row_tile(t, ROUTE_TILE)
    nt = t // tm
    slots = PEER_HEADS * PEER_TOPK
    out = jax.ShapeDtypeStruct((t, slots), F32)
    ospec = pl.BlockSpec((tm, slots), lambda s: (jnp.maximum(s - 1, 0), 0))
    full = pltpu.VMEM((slots, tm), F32)
    queries = pltpu.VMEM((2 * PEER_HEADS, tm, HEAD_DIM), F32)
    return pl.pallas_call(
        _query_route_kernel,
        out_shape=(out, out, out),
        grid=(nt + 1,),
        in_specs=[pl.BlockSpec((tm, d), lambda s: (jnp.minimum(s, nt - 1), 0)),
                  pl.BlockSpec(wq.shape, lambda s: (0, 0, 0), pipeline_mode=pl.Buffered(1)),
                  pl.BlockSpec(sub_keys.shape, lambda s: (0, 0, 0, 0))],
        out_specs=(ospec, ospec, ospec),
        scratch_shapes=[queries, queries, full, full, full],
        compiler_params=_params(("arbitrary",)),
        name="peer_query_route",
    )(x, wq, sub_keys)


def _coef_kernel(i_ref, j_ref, g_ref, c_ref, stage):
    tm = i_ref.shape[0]
    sub = lax.broadcasted_iota(jnp.int32, (N_KEYS, i_ref.shape[1]), 0).astype(F32)

    def group(t0, buf):
        ib = i_ref[pl.ds(t0, COEF_GROUP), :]
        jb = j_ref[pl.ds(t0, COEF_GROUP), :]
        gb = g_ref[pl.ds(t0, COEF_GROUP), :]
        for t in range(COEF_GROUP):
            ptg = jnp.where(sub == ib[t:t + 1, :], gb[t:t + 1, :], 0.0).astype(BF16)
            qt = jnp.where(sub == jb[t:t + 1, :], 1.0, 0.0).astype(BF16)
            buf[t * COEF_PITCH:t * COEF_PITCH + N_KEYS, :] = _dot_nt(ptg, qt)
        for i in range(N_KEYS):
            c_ref[i, pl.ds(t0, COEF_GROUP), :] = buf[pl.ds(i, COEF_GROUP, stride=COEF_PITCH), :].astype(c_ref.dtype)

    def trip(gi, carry):
        for u in range(COEF_UNROLL):
            group(pl.multiple_of((gi * COEF_UNROLL + u) * COEF_GROUP, COEF_GROUP), stage.at[u])
        return carry

    lax.fori_loop(0, tm // (COEF_GROUP * COEF_UNROLL), trip, 0)


def peer_coefficients(ki, kj, gate):
    t, slots = ki.shape
    tm = _row_tile(t, COEF_TILE)
    spec = pl.BlockSpec((tm, slots), lambda i: (i, 0))
    return pl.pallas_call(
        _coef_kernel,
        out_shape=jax.ShapeDtypeStruct((N_KEYS, t, N_KEYS), BF16),
        grid=(t // tm,),
        in_specs=[spec, spec, spec],
        out_specs=pl.BlockSpec((N_KEYS, tm, N_KEYS), lambda i: (0, i, 0)),
        scratch_shapes=[pltpu.VMEM((COEF_UNROLL, COEF_GROUP * COEF_PITCH, N_KEYS), F32)],
        compiler_params=_params(("parallel",)),
        name="peer_coefficients",
    )(ki, kj, gate)


def _gelu(x):
    return 0.5 * x * (1.0 + lax.erf(x * 0.7071067811865476))


def _peer_kernel(x_ref, u_ref, v_ref, c_ref, o_ref):
    @pl.when(pl.program_id(1) == 0)
    def _():
        o_ref[...] = jnp.zeros_like(o_ref)

    act = _gelu(_dot_nt(x_ref[...], u_ref[...]))
    w = jnp.concatenate([(c_ref[s].astype(F32) * act[:, s * N_KEYS:(s + 1) * N_KEYS]).astype(BF16)
                         for s in range(c_ref.shape[0])], axis=1)
    o_ref[...] += jnp.dot(w, v_ref[...], preferred_element_type=F32)


def peer_dense(x, u, v, coef):
    t, d = x.shape
    e = u.shape[0]
    tm, te = _row_tile(t, PEER_ROWS), PEER_EXPERTS
    return pl.pallas_call(
        _peer_kernel,
        out_shape=jax.ShapeDtypeStruct((t, d), F32),
        grid=(t // tm, e // te),
        in_specs=[pl.BlockSpec((tm, d), lambda i, j: (i, 0), pipeline_mode=pl.Buffered(1)),
                  pl.BlockSpec((te, d), lambda i, j: (j, 0)),
                  pl.BlockSpec((te, d), lambda i, j: (j, 0)),
                  pl.BlockSpec((te // N_KEYS, tm, N_KEYS), lambda i, j: (j, i, 0))],
        out_specs=pl.BlockSpec((tm, d), lambda i, j: (i, 0), pipeline_mode=pl.Buffered(1)),
        compiler_params=_params(("parallel", "arbitrary")),
        name="peer_dense",
    )(x, u, v, coef)


def kernel(x_prompt, x_sample, w_in, w_out, norm_mix, norm_ffn, lb_logits, hgrn_norm, attn_norm, attn_sink,
           peer_query, peer_sub_keys, peer_u, peer_v, final_norm):
    depth = w_in.shape[0]
    d_model = x_prompt.shape[-1]
    seq_len = x_prompt.shape[1]
    assert x_sample.shape[1] == seq_len, "prompt and sample sequences are stacked and must share a length"
    n_seq = x_prompt.shape[0] + x_sample.shape[0]
    xp = x_prompt.reshape(-1, d_model)
    xs = x_sample.reshape(-1, d_model)
    t_prompt, t_sample = xp.shape[0], xs.shape[0]
    hw = N_HGRN_HEADS * HEAD_DIM
    q_col = 5 * hw
    k_col = q_col + N_ATTN_HEADS * HEAD_DIM
    v_col = k_col + N_KV_HEADS * HEAD_DIM

    cos, sin_signed = rope_tables(seq_len)
    lbl_t = jnp.transpose(lb_logits.astype(F32), (1, 0, 2))

    h = rmsnorm_rows2(xp, xs, norm_mix[0], BF16)
    x = None
    for l in range(depth):
        p, u_bf16, v_bf16, w_out_bf16 = matmul(h, cast_bf16(w_in, l), side_casts=(peer_u, peer_v, w_out), layer=l)
        o_h = hgrn2(p, lbl_t, hgrn_norm[l], l, n_seq, seq_len)
        o_a = window_attention(p, cos, sin_signed, attn_sink[l].astype(F32), attn_norm[l], n_seq, seq_len,
                               q_col, k_col, v_col)
        if l == 0:
            x = matmul2_residual(o_h, o_a, w_out_bf16, xp, xs)
        else:
            x = matmul2_residual(o_h, o_a, w_out_bf16, x)
        h2 = rmsnorm_rows(x, norm_ffn[l], BF16)
        ki, kj, gate = peer_query_route(h2, cast_bf16(peer_query, l, col_panels=True), peer_sub_keys[l])
        coef = peer_coefficients(ki, kj, gate)
        po = peer_dense(h2, u_bf16, v_bf16, coef)
        if l + 1 < depth:
            x, h = add_rmsnorm_rows(x, po, norm_mix[l + 1], BF16)
    y_prompt = add_rmsnorm_final(x, po, final_norm, 0, t_prompt)
    y_sample = add_rmsnorm_final(x, po, final_norm, t_prompt, t_sample)
    return (y_prompt.reshape(x_prompt.shape), y_sample.reshape(x_sample.shape))
```

```python
import functools

import jax
import jax.numpy as jnp
from jax import lax
from jax.experimental import pallas as pl
from jax.experimental.pallas import tpu as pltpu

F32 = jnp.float32
BF16 = jnp.bfloat16

LANES = 128
SUBLANES = 8
BF16_ROWS = 2 * SUBLANES
HEAD_DIM = LANES
N_HGRN_HEADS = 16
N_ATTN_HEADS = 16
N_KV_HEADS = 4
GQA_REP = N_ATTN_HEADS // N_KV_HEADS
WINDOW = 128
ROPE_THETA = 10000.0
HGRN_SUB = 16
N_KEYS = 128
PEER_HEADS = 8
PEER_TOPK = 16
EPS = 1e-6
NEG_INF = -1e30
LOG2_E = 1.4426950408889634

VMEM_LIMIT = 56 * 1024 * 1024

NORM_ROWS = 256
MM_ROWS = 1024
MM_COLS = 512
CAST_BLOCK = 1024
HGRN_BLOCK = 128
HGRN_GROUP = 8
ATTN_BLOCK = WINDOW
ROUTE_TILE = 256
COEF_TILE = 256
COEF_GROUP = 16
COEF_PITCH = N_KEYS + SUBLANES
COEF_UNROLL = 8
PEER_ROWS = 1024
PEER_EXPERTS = 512


def _params(sem):
    return pltpu.CompilerParams(dimension_semantics=sem, vmem_limit_bytes=VMEM_LIMIT)


def _sigmoid(x):
    return 1.0 / (1.0 + jnp.exp(-x))


def _dot_nt(a, b):
    return lax.dot_general(a, b, (((1,), (1,)), ((), ())), preferred_element_type=F32)


def _row_tile(t, want):
    return want if t % want == 0 else t


def _common_tile(want, *sizes):
    while any(n % want for n in sizes):
        want //= 2
    return want


def _rms(x, g):
    return x * lax.rsqrt(jnp.mean(x * x, axis=-1, keepdims=True) + EPS) * g


def _cast_kernel(w_ref, o_ref):
    o_ref[...] = w_ref[...].astype(o_ref.dtype)


def cast_bf16(w, layer, col_panels=False):
    _, r, c = w.shape
    br, bc = _row_tile(r, CAST_BLOCK), _row_tile(c, CAST_BLOCK)
    if col_panels:
        out_shape = jax.ShapeDtypeStruct((c // bc, r, bc), BF16)
        out_spec = pl.BlockSpec((None, br, bc), lambda i, j: (j, i, 0))
    else:
        out_shape = jax.ShapeDtypeStruct((r, c), BF16)
        out_spec = pl.BlockSpec((br, bc), lambda i, j: (i, j))
    return pl.pallas_call(
        _cast_kernel,
        out_shape=out_shape,
        grid=(r // br, c // bc),
        in_specs=[pl.BlockSpec((None, br, bc), lambda i, j: (layer, i, j))],
        out_specs=out_spec,
        compiler_params=_params(("parallel", "parallel")),
        name="cast_bf16",
    )(w)


def _norm2_kernel(n_a, xa_ref, xb_ref, g_ref, h_ref):
    x = jnp.where(pl.program_id(0) < n_a, xa_ref[...], xb_ref[...])
    h_ref[...] = _rms(x, g_ref[...]).astype(h_ref.dtype)


def _norm_kernel(x_ref, g_ref, h_ref):
    h_ref[...] = _rms(x_ref[...], g_ref[...]).astype(h_ref.dtype)


def _addnorm_kernel(x_ref, y_ref, g_ref, s_ref, h_ref):
    x = x_ref[...] + y_ref[...]
    s_ref[...] = x
    h_ref[...] = _rms(x, g_ref[...]).astype(h_ref.dtype)


def _addnorm_final_kernel(x_ref, y_ref, g_ref, h_ref):
    h_ref[...] = _rms(x_ref[...] + y_ref[...], g_ref[...]).astype(h_ref.dtype)


def _two_source_specs(ta, tm, width):
    n_a = ta // tm
    spec_a = pl.BlockSpec((tm, width), lambda i, *_: (jnp.minimum(i, n_a - 1), 0))
    spec_b = pl.BlockSpec((tm, width), lambda i, *_: (jnp.maximum(i - n_a, 0), 0))
    return n_a, spec_a, spec_b


def rmsnorm_rows2(xa, xb, g, out_dtype):
    ta, d = xa.shape
    t = ta + xb.shape[0]
    tm = _common_tile(NORM_ROWS, ta, t)
    n_a, spec_a, spec_b = _two_source_specs(ta, tm, d)
    return pl.pallas_call(
        functools.partial(_norm2_kernel, n_a),
        out_shape=jax.ShapeDtypeStruct((t, d), out_dtype),
        grid=(t // tm,),
        in_specs=[spec_a, spec_b, pl.BlockSpec((1, d), lambda i: (0, 0))],
        out_specs=pl.BlockSpec((tm, d), lambda i: (i, 0)),
        compiler_params=_params(("parallel",)),
        name="rmsnorm2",
    )(xa, xb, g.reshape(1, d))


def rmsnorm_rows(x, g, out_dtype):
    t, d = x.shape
    tm = _row_tile(t, NORM_ROWS)
    row = pl.BlockSpec((tm, d), lambda i: (i, 0))
    return pl.pallas_call(
        _norm_kernel,
        out_shape=jax.ShapeDtypeStruct((t, d), out_dtype),
        grid=(t // tm,),
        in_specs=[row, pl.BlockSpec((1, d), lambda i: (0, 0))],
        out_specs=row,
        compiler_params=_params(("parallel",)),
        name="rmsnorm",
    )(x, g.reshape(1, d))


def add_rmsnorm_rows(x, y, g, out_dtype):
    t, d = x.shape
    tm = _row_tile(t, NORM_ROWS)
    row = pl.BlockSpec((tm, d), lambda i: (i, 0))
    return pl.pallas_call(
        _addnorm_kernel,
        out_shape=(jax.ShapeDtypeStruct((t, d), F32), jax.ShapeDtypeStruct((t, d), out_dtype)),
        grid=(t // tm,),
        in_specs=[row, row, pl.BlockSpec((1, d), lambda i: (0, 0))],
        out_specs=(row, row),
        compiler_params=_params(("parallel",)),
        name="add_rmsnorm",
    )(x, y, g.reshape(1, d))


def add_rmsnorm_final(x, y, g, row_start, n_rows):
    d = x.shape[1]
    tm = _row_tile(n_rows, NORM_ROWS)
    off = row_start // tm
    src = pl.BlockSpec((tm, d), lambda i: (i + off, 0))
    return pl.pallas_call(
        _addnorm_final_kernel,
        out_shape=jax.ShapeDtypeStruct((n_rows, d), F32),
        grid=(n_rows // tm,),
        in_specs=[src, src, pl.BlockSpec((1, d), lambda i: (0, 0))],
        out_specs=pl.BlockSpec((tm, d), lambda i: (i, 0)),
        compiler_params=_params(("parallel",)),
        name="add_rmsnorm_final",
    )(x, y, g.reshape(1, d))


def _mm_cast_kernel(n_side, a_ref, b_ref, *refs):
    o_ref = refs[n_side]
    o_ref[...] = jnp.dot(a_ref[...], b_ref[...], preferred_element_type=F32)
    for w_ref, c_ref in zip(refs[:n_side], refs[n_side + 1:]):
        c_ref[...] = w_ref[...].astype(c_ref.dtype)


def matmul(a, b, side_casts=(), layer=0):
    t, k = a.shape
    n = b.shape[1]
    tm, tn = _row_tile(t, MM_ROWS), MM_COLS
    gi, gj = t // tm, n // tn
    in_specs = [pl.BlockSpec((tm, k), lambda i, j: (i, 0)), pl.BlockSpec((k, tn), lambda i, j: (0, j))]
    out_shape = [jax.ShapeDtypeStruct((t, n), F32)]
    out_specs = [pl.BlockSpec((tm, tn), lambda i, j: (i, j))]
    for w in side_casts:
        _, r, c = w.shape
        rows = BF16_ROWS
        while rows * gi * gj < r:
            rows *= 2
        assert r % rows == 0
        last = r // rows - 1
        in_specs.append(pl.BlockSpec((None, rows, c),
                                     lambda i, j, last=last: (layer, jnp.minimum(i * gj + j, last), 0)))
        out_shape.append(jax.ShapeDtypeStruct((r, c), BF16))
        out_specs.append(pl.BlockSpec((rows, c), lambda i, j, last=last: (jnp.minimum(i * gj + j, last), 0)))
    res = pl.pallas_call(
        functools.partial(_mm_cast_kernel, len(side_casts)),
        out_shape=tuple(out_shape),
        grid=(gi, gj),
        in_specs=in_specs,
        out_specs=tuple(out_specs),
        compiler_params=_params(("arbitrary", "arbitrary")),
        name="matmul",
    )(a, b, *side_casts)
    return res if side_casts else res[0]


def _mm2_res_kernel(n_a, n_cols, a1_ref, a2_ref, b1_ref, b2_ref, g_ref, *refs):
    res_refs, (x_ref, xg_ref, r_ref, ss) = refs[:-4], refs[-4:]
    j = pl.program_id(1)
    acc = jnp.dot(a1_ref[...], b1_ref[...], preferred_element_type=F32)
    acc = acc + jnp.dot(a2_ref[...], b2_ref[...], preferred_element_type=F32)
    if len(res_refs) == 1:
        x = acc + res_refs[0][...]
    else:
        x = acc + jnp.where(pl.program_id(0) < n_a, res_refs[0][...], res_refs[1][...])
    x_ref[...] = x
    xg_ref[...] = (x * g_ref[...]).astype(xg_ref.dtype)
    sq = x * x
    part = sq[:, :LANES]
    for s in range(1, sq.shape[1] // LANES):
        part = part + sq[:, s * LANES:(s + 1) * LANES]

    @pl.when(j == 0)
    def _():
        ss[...] = part

    @pl.when(j > 0)
    def _():
        ss[...] += part

    @pl.when(j == pl.num_programs(1) - 1)
    def _():
        total = jnp.sum(ss[...], axis=-1, keepdims=True)
        r_ref[...] = jnp.broadcast_to(lax.rsqrt(total * (1.0 / n_cols) + EPS), r_ref.shape)


def matmul2_residual_norm(a1, a2, b, gain, res, res_b=None):
    t, kh = a1.shape
    n = b.shape[1]
    tm = _row_tile(t, MM_ROWS) if res_b is None else _common_tile(MM_ROWS, res.shape[0], t)
    tn = MM_COLS
    specs = [
        pl.BlockSpec((tm, kh), lambda i, j: (i, 0)),
        pl.BlockSpec((tm, kh), lambda i, j: (i, 0)),
        pl.BlockSpec((kh, tn), lambda i, j: (0, j)),
        pl.BlockSpec((kh, tn), lambda i, j: (1, j)),
        pl.BlockSpec((1, tn), lambda i, j: (0, j)),
    ]
    tile = pl.BlockSpec((tm, tn), lambda i, j: (i, j))
    if res_b is None:
        n_a = 0
        specs.append(tile)
        args = (a1, a2, b, b, gain.reshape(1, n), res)
    else:
        n_a = res.shape[0] // tm
        specs.append(pl.BlockSpec((tm, tn), lambda i, j: (jnp.minimum(i, n_a - 1), j)))
        specs.append(pl.BlockSpec((tm, tn), lambda i, j: (jnp.maximum(i - n_a, 0), j)))
        args = (a1, a2, b, b, gain.reshape(1, n), res, res_b)
    return pl.pallas_call(
        functools.partial(_mm2_res_kernel, n_a, n),
        out_shape=(jax.ShapeDtypeStruct((t, n), F32), jax.ShapeDtypeStruct((t, n), BF16),
                   jax.ShapeDtypeStruct((t, LANES), F32)),
        grid=(t // tm, n // tn),
        in_specs=specs,
        out_specs=(tile, tile, pl.BlockSpec((tm, LANES), lambda i, j: (i, 0))),
        scratch_shapes=[pltpu.VMEM((tm, LANES), F32)],
        compiler_params=_params(("parallel", "arbitrary")),
        name="matmul2_residual_norm",
    )(*args)


def _hgrn_masks(cb, width, reverse):
    ti = lax.broadcasted_iota(jnp.int32, (cb, width), 0)
    ji = lax.broadcasted_iota(jnp.int32, (cb, width), 1) % HEAD_DIM
    same = (ti // HGRN_SUB) == (ji // HGRN_SUB)
    masks = [jnp.logical_and(same, (ji >= ti) if reverse else (ji <= ti))]
    hs = HGRN_SUB
    while 2 * hs <= cb:
        grp = (ti // (2 * hs)) == (ji // (2 * hs))
        t_hi = (ti % (2 * hs)) >= hs
        j_hi = (ji % (2 * hs)) >= hs
        if reverse:
            sel = jnp.logical_and(jnp.logical_not(t_hi), j_hi)
        else:
            sel = jnp.logical_and(t_hi, jnp.logical_not(j_hi))
        masks.append(jnp.logical_and(grp, sel))
        hs *= 2
    t1 = lax.broadcasted_iota(jnp.int32, (cb, cb), 0)
    j1 = lax.broadcasted_iota(jnp.int32, (cb, cb), 1)
    tri = ((j1 >= t1) if reverse else (j1 <= t1)).astype(BF16)
    return masks, tri


def _ref_rows(b, group, idx):
    cb, w = b.shape
    g = b.reshape(cb // group, group, w)
    r = jnp.broadcast_to(g[:, idx:idx + 1, :], g.shape)
    return r.reshape(cb, w)


def _cumsum_rows(tri, a):
    w = a.shape[1]
    a1 = a.astype(BF16)
    r1 = a - a1.astype(F32)
    a2 = r1.astype(BF16)
    a3 = (r1 - a2.astype(F32)).astype(BF16)
    out = jnp.dot(tri, jnp.concatenate([a1, a2, a3], axis=1), preferred_element_type=F32)
    return out[:, :w] + out[:, w:2 * w] + out[:, 2 * w:]


def _heads(x):
    return [x[:, h * HEAD_DIM:(h + 1) * HEAD_DIM] for h in range(x.shape[1] // HEAD_DIM)]


def _hgrn_group(qz, fz, v, lb, st_ref, first_head, masks, tri, reverse):
    cb = qz.shape[0]
    q = qz * _sigmoid(qz)
    f = lb + (1.0 - lb) * _sigmoid(fz)
    a = jnp.log(f) * LOG2_E
    k = 1.0 - f
    b = _cumsum_rows(tri, a)
    b_tot = b[0:1, :] if reverse else b[cb - 1:cb, :]

    def scores(eq, ek):
        qs, ks = _heads((q * eq).astype(BF16)), _heads((k * ek).astype(BF16))
        return jnp.concatenate([_dot_nt(qh, kh) for qh, kh in zip(qs, ks)], axis=1)

    r0 = _ref_rows(b, HGRN_SUB, HGRN_SUB // 2 if reverse else HGRN_SUB // 2 - 1)
    att = jnp.where(masks[0], scores(jnp.exp2(b - r0), jnp.exp2(r0 - b)), 0.0)
    hs = HGRN_SUB
    for mask in masks[1:]:
        e = jnp.exp2(-jnp.abs(b - _ref_rows(b, 2 * hs, hs if reverse else hs - 1)))
        att = jnp.where(mask, scores(e, e), att)
        hs *= 2
    att_h = _heads(att.astype(BF16))
    v_h = _heads(v)
    q_dec = _heads((q * jnp.exp2(b)).astype(BF16))
    k_dec = _heads((k * jnp.exp2(b_tot - b)).astype(BF16))
    decay = _heads(jnp.exp2(b_tot))
    outs = []
    for h in range(len(v_h)):
        st = st_ref[first_head + h]
        o = jnp.dot(att_h[h], v_h[h].astype(BF16), preferred_element_type=F32)
        outs.append(o + _dot_nt(q_dec[h], st.astype(BF16)))
        st_ref[first_head + h] = st * decay[h] + jnp.dot(v_h[h].T.astype(BF16), k_dec[h],
                                                         preferred_element_type=F32)
    return jnp.concatenate(outs, axis=1)


def _lower_bound(lbl, layer):
    if layer == 0:
        return jnp.zeros((1, lbl.shape[1]), F32)
    m = jnp.max(lbl, axis=0, keepdims=True)
    e = jnp.exp(lbl - m)
    sm = e / jnp.sum(e, axis=0, keepdims=True)
    return jnp.sum(sm[1:layer + 1, :], axis=0, keepdims=True)


def _hgrn_fwd_kernel(layer, q_ref, f_ref, i_ref, lbl_ref, o_ref, st_ref):
    @pl.when(pl.program_id(1) == 0)
    def _():
        st_ref[...] = jnp.zeros_like(st_ref)

    gw = HGRN_GROUP * HEAD_DIM
    masks, tri = _hgrn_masks(q_ref.shape[0], gw, False)
    lb = _lower_bound(lbl_ref[...], layer)
    for g in range(N_HGRN_HEADS // HGRN_GROUP):
        sl = slice(g * gw, (g + 1) * gw)
        o_ref[:, sl] = _hgrn_group(q_ref[:, sl], f_ref[:, sl], i_ref[:, sl], lb[:, sl], st_ref,
                                   g * HGRN_GROUP, masks, tri, False)


def _hgrn_bwd_kernel(layer, q_ref, f_ref, i_ref, g_ref, of_ref, lbl_ref, nw_ref, o_ref, st_ref):
    @pl.when(pl.program_id(1) == 0)
    def _():
        st_ref[...] = jnp.zeros_like(st_ref)

    gw = HGRN_GROUP * HEAD_DIM
    masks, tri = _hgrn_masks(q_ref.shape[0], gw, True)
    lb = _lower_bound(lbl_ref[...], layer)
    for g in range(N_HGRN_HEADS // HGRN_GROUP):
        sl = slice(g * gw, (g + 1) * gw)
        o = _hgrn_group(q_ref[:, sl], f_ref[:, sl], i_ref[:, sl], lb[:, sl], st_ref,
                        g * HGRN_GROUP, masks, tri, True) + of_ref[:, sl]
        o = jnp.concatenate([oh * lax.rsqrt(jnp.mean(oh * oh, axis=-1, keepdims=True) + EPS)
                             for oh in _heads(o)], axis=1) * nw_ref[:, sl]
        gz = g_ref[:, sl]
        o_ref[:, sl] = (o * (gz * _sigmoid(gz))).astype(o_ref.dtype)


def hgrn2(p, lb_logits_t, norm_w, layer, n_seq, seq_len):
    t = p.shape[0]
    w = N_HGRN_HEADS * HEAD_DIM
    cb = HGRN_BLOCK
    nb = seq_len // cb
    depth = lb_logits_t.shape[1]

    def col(j, reverse):
        if reverse:
            return pl.BlockSpec((cb, w), lambda s, c: (s * nb + nb - 1 - c, j))
        return pl.BlockSpec((cb, w), lambda s, c: (s * nb + c, j))

    def lbl(d):
        return pl.BlockSpec((None, depth, w), lambda s, c: (d, 0, 0))

    scratch = [pltpu.VMEM((N_HGRN_HEADS, HEAD_DIM, HEAD_DIM), F32)]
    o_f = pl.pallas_call(
        functools.partial(_hgrn_fwd_kernel, layer),
        out_shape=jax.ShapeDtypeStruct((t, w), F32),
        grid=(n_seq, nb),
        in_specs=[col(0, False), col(1, False), col(3, False), lbl(0)],
        out_specs=col(0, False),
        scratch_shapes=scratch,
        compiler_params=_params(("parallel", "arbitrary")),
        name="hgrn_fwd",
    )(p, p, p, lb_logits_t)
    return pl.pallas_call(
        functools.partial(_hgrn_bwd_kernel, layer),
        out_shape=jax.ShapeDtypeStruct((t, w), BF16),
        grid=(n_seq, nb),
        in_specs=[col(0, True), col(2, True), col(3, True), col(4, True), col(0, True), lbl(1),
                  pl.BlockSpec((1, w), lambda s, c: (0, 0))],
        out_specs=col(0, True),
        scratch_shapes=scratch,
        compiler_params=_params(("parallel", "arbitrary")),
        name="hgrn_bwd",
    )(p, p, p, p, o_f, lb_logits_t, norm_w.reshape(1, w))


def _rope(x, cos, sin_signed):
    return x * cos + pltpu.roll(x, HEAD_DIM // 2, axis=1) * sin_signed


def _attn_kernel(q_ref, kp_ref, kc_ref, kn_ref, vp_ref, vc_ref, vn_ref,
                 cp_ref, sp_ref, cc_ref, sc_ref, cn_ref, sn_ref, sink_ref, nw_ref, o_ref):
    n = pl.program_id(1)
    nb = pl.num_programs(1)
    tb = q_ref.shape[0]
    rows = GQA_REP * tb
    ri = lax.broadcasted_iota(jnp.int32, (rows, tb), 0) % tb
    ci = lax.broadcasted_iota(jnp.int32, (rows, tb), 1)
    head_of_row = lax.broadcasted_iota(jnp.int32, (rows, 1), 0) // tb
    m_prev = jnp.logical_and(ci >= ri, n > 0)
    m_next = jnp.logical_and(ci <= ri, n < nb - 1)
    scale = HEAD_DIM ** -0.5
    cc, sc = cc_ref[...], sc_ref[...]
    for g in range(N_KV_HEADS):
        sl = slice(g * HEAD_DIM, (g + 1) * HEAD_DIM)
        kcat = jnp.concatenate([
            _rope(kp_ref[:, sl], cp_ref[...], sp_ref[...]),
            _rope(kc_ref[:, sl], cc, sc),
            _rope(kn_ref[:, sl], cn_ref[...], sn_ref[...]),
        ], axis=0).astype(BF16)
        vcat = jnp.concatenate([vp_ref[:, sl], vc_ref[:, sl], vn_ref[:, sl]], axis=0).astype(BF16)
        heads = [g * GQA_REP + r for r in range(GQA_REP)]
        q = jnp.concatenate([_rope(q_ref[:, h * HEAD_DIM:(h + 1) * HEAD_DIM], cc, sc) for h in heads],
                            axis=0).astype(BF16)
        s = _dot_nt(q, kcat) * scale
        s = jnp.concatenate([jnp.where(m_prev, s[:, :tb], NEG_INF), s[:, tb:2 * tb],
                             jnp.where(m_next, s[:, 2 * tb:], NEG_INF)], axis=1)
        sink = jnp.zeros((rows, 1), F32)
        for r, h in enumerate(heads):
            sink = jnp.where(head_of_row == r, sink_ref[h], sink)
        m = jnp.maximum(jnp.max(s, axis=-1, keepdims=True), sink)
        e = jnp.exp(s - m)
        den = jnp.sum(e, axis=-1, keepdims=True) + jnp.exp(sink - m)
        o = jnp.dot(e.astype(BF16), vcat, preferred_element_type=F32) / den
        o = o * lax.rsqrt(jnp.mean(o * o, axis=-1, keepdims=True) + EPS)
        for r, h in enumerate(heads):
            hs = slice(h * HEAD_DIM, (h + 1) * HEAD_DIM)
            o_ref[:, hs] = (o[r * tb:(r + 1) * tb, :] * nw_ref[:, hs]).astype(o_ref.dtype)


def window_attention(p, cos, sin_signed, sink, norm_w, n_seq, seq_len, q_col, k_col, v_col):
    t = p.shape[0]
    tb = ATTN_BLOCK
    nb = seq_len // tb
    qw = N_ATTN_HEADS * HEAD_DIM
    kw = N_KV_HEADS * HEAD_DIM

    def prev(n):
        return jnp.maximum(n - 1, 0)

    def nxt(n):
        return jnp.minimum(n + 1, nb - 1)

    def same(n):
        return n

    def kv(colblk, f):
        return pl.BlockSpec((tb, kw), lambda s, n: (s * nb + f(n), colblk))

    def tab(f):
        return pl.BlockSpec((tb, HEAD_DIM), lambda s, n: (f(n), 0))

    qspec = pl.BlockSpec((tb, qw), lambda s, n: (s * nb + n, q_col // qw))
    kb, vb = k_col // kw, v_col // kw
    return pl.pallas_call(
        _attn_kernel,
        out_shape=jax.ShapeDtypeStruct((t, qw), BF16),
        grid=(n_seq, nb),
        in_specs=[qspec, kv(kb, prev), kv(kb, same), kv(kb, nxt), kv(vb, prev), kv(vb, same), kv(vb, nxt),
                  tab(prev), tab(prev), tab(same), tab(same), tab(nxt), tab(nxt),
                  pl.BlockSpec(memory_space=pltpu.SMEM),
                  pl.BlockSpec((1, qw), lambda s, n: (0, 0))],
        out_specs=pl.BlockSpec((tb, qw), lambda s, n: (s * nb + n, 0)),
        compiler_params=_params(("parallel", "parallel")),
        name="window_attention",
    )(p, p, p, p, p, p, p, cos, sin_signed, cos, sin_signed, cos, sin_signed, sink, norm_w.reshape(1, qw))


def rope_tables(seq_len):
    inv_freq = ROPE_THETA ** (-jnp.arange(0, HEAD_DIM, 2, dtype=F32) / HEAD_DIM)
    ang = jnp.arange(seq_len, dtype=F32)[:, None] * inv_freq[None, :]
    ang = jnp.concatenate([ang, ang], axis=-1)
    sign = jnp.concatenate([-jnp.ones((HEAD_DIM // 2,), F32), jnp.ones((HEAD_DIM // 2,), F32)])
    return jnp.cos(ang), jnp.sin(ang) * sign[None, :]


assert PEER_TOPK == 2 * SUBLANES
_PAIR_LIMIT = [PEER_TOPK // (a + 1) for a in range(PEER_TOPK)]
_N_MID = PEER_TOPK // 2 - 1
_CAND_ROWS = PEER_TOPK + SUBLANES * _N_MID + SUBLANES
assert all(lim <= SUBLANES for lim in _PAIR_LIMIT[1:]) and all(lim == 1 for lim in _PAIR_LIMIT[_N_MID + 1:])


def _extract_top(s, iota, k_out):
    n = s.shape[0]
    vals, idxs = [], []
    for _ in range(k_out):
        m = jnp.max(s, axis=0, keepdims=True)
        idx = jnp.min(jnp.where(s == m, iota, float(n)), axis=0, keepdims=True)
        s = jnp.where(iota == idx, -jnp.inf, s)
        vals.append(m)
        idxs.append(idx)
    return vals, idxs


def _query_route_kernel(x_ref, r_ref, wq_ref, sk_ref, i_ref, j_ref, g_ref, qy_a, qy_b, ti, tj, tg):
    tm = x_ref.shape[0]
    step = pl.program_id(0)
    slabs = wq_ref.shape[2] // HEAD_DIM
    heads_per_trip = slabs // 2
    kio = lax.broadcasted_iota(jnp.int32, (N_KEYS, LANES), 0).astype(F32)
    pio = lax.broadcasted_iota(jnp.int32, (_CAND_ROWS, LANES), 0).astype(F32)
    sub = lax.broadcasted_iota(jnp.int32, (SUBLANES, LANES), 0)

    @pl.when(step == 0)
    def _():
        qy_b[...] = jnp.zeros_like(qy_b)

    halves = tm // LANES
    assert heads_per_trip * halves == slabs, "one projection slab per (head, token half) unit"

    def unit_scores(h, half, q_read):
        tok = slice(half * LANES, (half + 1) * LANES)
        return [_dot_nt(sk_ref[h, c].astype(BF16), q_read[2 * h + c, tok, :].astype(BF16)) for c in range(2)]

    def unit_route(h, half, scores):
        row = pl.multiple_of(h * PEER_TOPK, PEER_TOPK)
        tok = slice(half * LANES, (half + 1) * LANES)
        (v0, i0), (v1, i1) = [_extract_top(s, kio, PEER_TOPK) for s in scores]
        v1_lo = jnp.concatenate(v1[:SUBLANES], axis=0)
        groups = [v0[0] + jnp.concatenate(v1, axis=0)]
        for a in range(1, _N_MID + 1):
            groups.append(jnp.where(sub < _PAIR_LIMIT[a], v0[a] + v1_lo, -jnp.inf))
        groups.append(jnp.concatenate(v0[_N_MID + 1:], axis=0) + v1[0])
        top_v, pos = _extract_top(jnp.concatenate(groups, axis=0), pio, PEER_TOPK)
        top_v, pos = jnp.concatenate(top_v, axis=0), jnp.concatenate(pos, axis=0)
        mid = jnp.floor((pos - PEER_TOPK) * (1.0 / SUBLANES))
        last = float(PEER_TOPK + SUBLANES * _N_MID)
        a_sel = jnp.where(pos < PEER_TOPK, 0.0, jnp.where(pos < last, 1.0 + mid, pos - (last - _N_MID - 1)))
        b_sel = jnp.where(pos < PEER_TOPK, pos,
                          jnp.where(pos < last, pos - PEER_TOPK - SUBLANES * mid, 0.0))
        ki = jnp.zeros_like(pos)
        kj = jnp.zeros_like(pos)
        for a in range(PEER_TOPK):
            ki = ki + jnp.where(a_sel == float(a), i0[a], 0.0)
            kj = kj + jnp.where(b_sel == float(a), i1[a], 0.0)
        e = jnp.exp(top_v - top_v[0:1, :])
        ti[pl.ds(row, PEER_TOPK), tok] = ki
        tj[pl.ds(row, PEER_TOPK), tok] = kj
        tg[pl.ds(row, PEER_TOPK), tok] = e / jnp.sum(e, axis=0, keepdims=True)

    def run(q_write, q_read):
        def trip(g, carry):
            units = [(g * heads_per_trip + u, half) for u in range(heads_per_trip) for half in range(halves)]
            x = x_ref[...]
            r = r_ref[...]
            scores = []
            for s, (h, half) in enumerate(units):
                scores.append(unit_scores(h, half, q_read))
                if s % 2 == 1:
                    acc = jnp.dot(x, wq_ref[g, :, (s - 1) * HEAD_DIM:(s + 1) * HEAD_DIM],
                                  preferred_element_type=F32)
                    q_write[g * slabs + s - 1] = acc[:, :HEAD_DIM] * r
                    q_write[g * slabs + s] = acc[:, HEAD_DIM:] * r
            for (h, half), sc in zip(units, scores):
                unit_route(h, half, sc)
            return carry

        lax.fori_loop(0, wq_ref.shape[0], trip, 0)

    @pl.when(lax.rem(step, 2) == 0)
    def _():
        run(qy_a, qy_b)

    @pl.when(lax.rem(step, 2) == 1)
    def _():
        run(qy_b, qy_a)

    i_ref[...] = ti[...].T
    j_ref[...] = tj[...].T
    g_ref[...] = tg[...].T


def peer_query_route(x, r, wq, sub_keys):
    t, d = x.shape
    tm = _row_tile(t, ROUTE_TILE)
    nt = t // tm
    slots = PEER_HEADS * PEER_TOPK
    out = jax.ShapeDtypeStruct((t, slots), F32)
    ospec = pl.BlockSpec((tm, slots), lambda s: (jnp.maximum(s - 1, 0), 0))
    full = pltpu.VMEM((slots, tm), F32)
    queries = pltpu.VMEM((2 * PEER_HEADS, tm, HEAD_DIM), F32)
    return pl.pallas_call(
        _query_route_kernel,
        out_shape=(out, out, out),
        grid=(nt + 1,),
        in_specs=[pl.BlockSpec((tm, d), lambda s: (jnp.minimum(s, nt - 1), 0)),
                  pl.BlockSpec((tm, LANES), lambda s: (jnp.minimum(s, nt - 1), 0)),
                  pl.BlockSpec(wq.shape, lambda s: (0, 0, 0), pipeline_mode=pl.Buffered(1)),
                  pl.BlockSpec(sub_keys.shape, lambda s: (0, 0, 0, 0))],
        out_specs=(ospec, ospec, ospec),
        scratch_shapes=[queries, queries, full, full, full],
        compiler_params=_params(("arbitrary",)),
        name="peer_query_route",
    )(x, r, wq, sub_keys)


def _coef_kernel(i_ref, j_ref, g_ref, c_ref, stage):
    tm = i_ref.shape[0]
    sub = lax.broadcasted_iota(jnp.int32, (N_KEYS, i_ref.shape[1]), 0).astype(F32)

    def group(t0, buf):
        ib = i_ref[pl.ds(t0, COEF_GROUP), :]
        jb = j_ref[pl.ds(t0, COEF_GROUP), :]
        gb = g_ref[pl.ds(t0, COEF_GROUP), :]
        for t in range(COEF_GROUP):
            ptg = jnp.where(sub == ib[t:t + 1, :], gb[t:t + 1, :], 0.0).astype(BF16)
            qt = jnp.where(sub == jb[t:t + 1, :], 1.0, 0.0).astype(BF16)
            buf[t * COEF_PITCH:t * COEF_PITCH + N_KEYS, :] = _dot_nt(ptg, qt)
        for i in range(N_KEYS):
            c_ref[i, pl.ds(t0, COEF_GROUP), :] = buf[pl.ds(i, COEF_GROUP, stride=COEF_PITCH), :].astype(c_ref.dtype)

    def trip(gi, carry):
        for u in range(COEF_UNROLL):
            group(pl.multiple_of((gi * COEF_UNROLL + u) * COEF_GROUP, COEF_GROUP), stage.at[u])
        return carry

    lax.fori_loop(0, tm // (COEF_GROUP * COEF_UNROLL), trip, 0)


def peer_coefficients(ki, kj, gate):
    t, slots = ki.shape
    tm = _row_tile(t, COEF_TILE)
    spec = pl.BlockSpec((tm, slots), lambda i: (i, 0))
    return pl.pallas_call(
        _coef_kernel,
        out_shape=jax.ShapeDtypeStruct((N_KEYS, t, N_KEYS), BF16),
        grid=(t // tm,),
        in_specs=[spec, spec, spec],
        out_specs=pl.BlockSpec((N_KEYS, tm, N_KEYS), lambda i: (0, i, 0)),
        scratch_shapes=[pltpu.VMEM((COEF_UNROLL, COEF_GROUP * COEF_PITCH, N_KEYS), F32)],
        compiler_params=_params(("parallel",)),
        name="peer_coefficients",
    )(ki, kj, gate)


def _gelu(x):
    return 0.5 * x * (1.0 + lax.erf(x * 0.7071067811865476))


def _peer_kernel(x_ref, r_ref, u_ref, v_ref, c_ref, o_ref):
    @pl.when(pl.program_id(1) == 0)
    def _():
        o_ref[...] = jnp.zeros_like(o_ref)

    hid = _dot_nt(x_ref[...], u_ref[...])
    r = r_ref[...]
    w = jnp.concatenate([(c_ref[s].astype(F32) * _gelu(hid[:, s * N_KEYS:(s + 1) * N_KEYS] * r)).astype(BF16)
                         for s in range(c_ref.shape[0])], axis=1)
    o_ref[...] += jnp.dot(w, v_ref[...], preferred_element_type=F32)


def peer_dense(x, r, u, v, coef):
    t, d = x.shape
    e = u.shape[0]
    tm, te = _row_tile(t, PEER_ROWS), PEER_EXPERTS
    return pl.pallas_call(
        _peer_kernel,
        out_shape=jax.ShapeDtypeStruct((t, d), F32),
        grid=(t // tm, e // te),
        in_specs=[pl.BlockSpec((tm, d), lambda i, j: (i, 0), pipeline_mode=pl.Buffered(1)),
                  pl.BlockSpec((tm, LANES), lambda i, j: (i, 0)),
                  pl.BlockSpec((te, d), lambda i, j: (j, 0)),
                  pl.BlockSpec((te, d), lambda i, j: (j, 0)),
                  pl.BlockSpec((te // N_KEYS, tm, N_KEYS), lambda i, j: (j, i, 0))],
        out_specs=pl.BlockSpec((tm, d), lambda i, j: (i, 0), pipeline_mode=pl.Buffered(1)),
        compiler_params=_params(("parallel", "arbitrary")),
        name="peer_dense",
    )(x, r, u, v, coef)


def kernel(x_prompt, x_sample, w_in, w_out, norm_mix, norm_ffn, lb_logits, hgrn_norm, attn_norm, attn_sink,
           peer_query, peer_sub_keys, peer_u, peer_v, final_norm):
    depth = w_in.shape[0]
    d_model = x_prompt.shape[-1]
    seq_len = x_prompt.shape[1]
    assert x_sample.shape[1] == seq_len, "prompt and sample sequences are stacked and must share a length"
    n_seq = x_prompt.shape[0] + x_sample.shape[0]
    xp = x_prompt.reshape(-1, d_model)
    xs = x_sample.reshape(-1, d_model)
    t_prompt, t_sample = xp.shape[0], xs.shape[0]
    hw = N_HGRN_HEADS * HEAD_DIM
    q_col = 5 * hw
    k_col = q_col + N_ATTN_HEADS * HEAD_DIM
    v_col = k_col + N_KV_HEADS * HEAD_DIM

    cos, sin_signed = rope_tables(seq_len)
    lbl_t = jnp.transpose(lb_logits.astype(F32), (1, 0, 2))

    h = rmsnorm_rows2(xp, xs, norm_mix[0], BF16)
    x = None
    for l in range(depth):
        p, u_bf16, v_bf16, w_out_bf16 = matmul(h, cast_bf16(w_in, l), side_casts=(peer_u, peer_v, w_out), layer=l)
        o_h = hgrn2(p, lbl_t, hgrn_norm[l], l, n_seq, seq_len)
        o_a = window_attention(p, cos, sin_signed, attn_sink[l].astype(F32), attn_norm[l], n_seq, seq_len,
                               q_col, k_col, v_col)
        if l == 0:
            x, xg, r = matmul2_residual_norm(o_h, o_a, w_out_bf16, norm_ffn[l], xp, xs)
        else:
            x, xg, r = matmul2_residual_norm(o_h, o_a, w_out_bf16, norm_ffn[l], x)
        ki, kj, gate = peer_query_route(xg, r, cast_bf16(peer_query, l, col_panels=True), peer_sub_keys[l])
        coef = peer_coefficients(ki, kj, gate)
        po = peer_dense(xg, r, u_bf16, v_bf16, coef)
        if l + 1 < depth:
            x, h = add_rmsnorm_rows(x, po, norm_mix[l + 1], BF16)
    y_prompt = add_rmsnorm_final(x, po, final_norm, 0, t_prompt)
    y_sample = add_rmsnorm_final(x, po, final_norm, t_prompt, t_sample)
    return (y_prompt.reshape(x_prompt.shape), y_sample.reshape(x_sample.shape))
```

```python
import functools

import jax
import jax.numpy as jnp
from jax import lax
from jax.experimental import pallas as pl
from jax.experimental.pallas import tpu as pltpu

F32 = jnp.float32
BF16 = jnp.bfloat16

LANES = 128
SUBLANES = 8
BF16_ROWS = 2 * SUBLANES
HEAD_DIM = LANES
N_HGRN_HEADS = 16
N_ATTN_HEADS = 16
N_KV_HEADS = 4
GQA_REP = N_ATTN_HEADS // N_KV_HEADS
WINDOW = 128
ROPE_THETA = 10000.0
HGRN_SUB = 16
N_KEYS = 128
PEER_HEADS = 8
PEER_TOPK = 16
EPS = 1e-6
NEG_INF = -1e30
LOG2_E = 1.4426950408889634

VMEM_LIMIT = 56 * 1024 * 1024

NORM_ROWS = 256
MM_ROWS = 1024
MM_COLS = 512
IN_PROJ_COLS = 1024
CAST_BLOCK = 1024
HGRN_BLOCK = 128
HGRN_GROUP = 8
ATTN_BLOCK = WINDOW
ROUTE_TILE = 256
COEF_TILE = 256
COEF_GROUP = 16
COEF_PITCH = N_KEYS + SUBLANES
COEF_UNROLL = 8
PEER_ROWS = 1024
PEER_EXPERTS = 512


def _params(sem):
    return pltpu.CompilerParams(dimension_semantics=sem, vmem_limit_bytes=VMEM_LIMIT)


def _sigmoid(x):
    return 1.0 / (1.0 + jnp.exp(-x))


def _dot_nt(a, b):
    return lax.dot_general(a, b, (((1,), (1,)), ((), ())), preferred_element_type=F32)


def _row_tile(t, want):
    return want if t % want == 0 else t


def _common_tile(want, *sizes):
    while any(n % want for n in sizes):
        want //= 2
    return want


def _rms(x, g):
    return x * lax.rsqrt(jnp.mean(x * x, axis=-1, keepdims=True) + EPS) * g


def _cast_kernel(w_ref, o_ref):
    o_ref[...] = w_ref[...].astype(o_ref.dtype)


def cast_bf16(w, layer, col_panels=False):
    _, r, c = w.shape
    br, bc = _row_tile(r, CAST_BLOCK), _row_tile(c, CAST_BLOCK)
    if col_panels:
        out_shape = jax.ShapeDtypeStruct((c // bc, r, bc), BF16)
        out_spec = pl.BlockSpec((None, br, bc), lambda i, j: (j, i, 0))
    else:
        out_shape = jax.ShapeDtypeStruct((r, c), BF16)
        out_spec = pl.BlockSpec((br, bc), lambda i, j: (i, j))
    return pl.pallas_call(
        _cast_kernel,
        out_shape=out_shape,
        grid=(r // br, c // bc),
        in_specs=[pl.BlockSpec((None, br, bc), lambda i, j: (layer, i, j))],
        out_specs=out_spec,
        compiler_params=_params(("parallel", "parallel")),
        name="cast_bf16",
    )(w)


def _norm2_kernel(n_a, xa_ref, xb_ref, g_ref, h_ref):
    x = jnp.where(pl.program_id(0) < n_a, xa_ref[...], xb_ref[...])
    h_ref[...] = _rms(x, g_ref[...]).astype(h_ref.dtype)


def _norm_kernel(x_ref, g_ref, h_ref):
    h_ref[...] = _rms(x_ref[...], g_ref[...]).astype(h_ref.dtype)


def _addnorm_kernel(x_ref, y_ref, g_ref, s_ref, h_ref):
    x = x_ref[...] + y_ref[...]
    s_ref[...] = x
    h_ref[...] = _rms(x, g_ref[...]).astype(h_ref.dtype)


def _addnorm_final_kernel(x_ref, y_ref, g_ref, h_ref):
    h_ref[...] = _rms(x_ref[...] + y_ref[...], g_ref[...]).astype(h_ref.dtype)


def _two_source_specs(ta, tm, width):
    n_a = ta // tm
    spec_a = pl.BlockSpec((tm, width), lambda i, *_: (jnp.minimum(i, n_a - 1), 0))
    spec_b = pl.BlockSpec((tm, width), lambda i, *_: (jnp.maximum(i - n_a, 0), 0))
    return n_a, spec_a, spec_b


def rmsnorm_rows2(xa, xb, g, out_dtype):
    ta, d = xa.shape
    t = ta + xb.shape[0]
    tm = _common_tile(NORM_ROWS, ta, t)
    n_a, spec_a, spec_b = _two_source_specs(ta, tm, d)
    return pl.pallas_call(
        functools.partial(_norm2_kernel, n_a),
        out_shape=jax.ShapeDtypeStruct((t, d), out_dtype),
        grid=(t // tm,),
        in_specs=[spec_a, spec_b, pl.BlockSpec((1, d), lambda i: (0, 0))],
        out_specs=pl.BlockSpec((tm, d), lambda i: (i, 0)),
        compiler_params=_params(("parallel",)),
        name="rmsnorm2",
    )(xa, xb, g.reshape(1, d))


def rmsnorm_rows(x, g, out_dtype):
    t, d = x.shape
    tm = _row_tile(t, NORM_ROWS)
    row = pl.BlockSpec((tm, d), lambda i: (i, 0))
    return pl.pallas_call(
        _norm_kernel,
        out_shape=jax.ShapeDtypeStruct((t, d), out_dtype),
        grid=(t // tm,),
        in_specs=[row, pl.BlockSpec((1, d), lambda i: (0, 0))],
        out_specs=row,
        compiler_params=_params(("parallel",)),
        name="rmsnorm",
    )(x, g.reshape(1, d))


def add_rmsnorm_rows(x, y, g, out_dtype):
    t, d = x.shape
    tm = _row_tile(t, NORM_ROWS)
    row = pl.BlockSpec((tm, d), lambda i: (i, 0))
    return pl.pallas_call(
        _addnorm_kernel,
        out_shape=(jax.ShapeDtypeStruct((t, d), F32), jax.ShapeDtypeStruct((t, d), out_dtype)),
        grid=(t // tm,),
        in_specs=[row, row, pl.BlockSpec((1, d), lambda i: (0, 0))],
        out_specs=(row, row),
        compiler_params=_params(("parallel",)),
        name="add_rmsnorm",
    )(x, y, g.reshape(1, d))


def add_rmsnorm_final(x, y, g, row_start, n_rows):
    d = x.shape[1]
    tm = _row_tile(n_rows, NORM_ROWS)
    off = row_start // tm
    src = pl.BlockSpec((tm, d), lambda i: (i + off, 0))
    return pl.pallas_call(
        _addnorm_final_kernel,
        out_shape=jax.ShapeDtypeStruct((n_rows, d), F32),
        grid=(n_rows // tm,),
        in_specs=[src, src, pl.BlockSpec((1, d), lambda i: (0, 0))],
        out_specs=pl.BlockSpec((tm, d), lambda i: (i, 0)),
        compiler_params=_params(("parallel",)),
        name="add_rmsnorm_final",
    )(x, y, g.reshape(1, d))


def _mm_cast_kernel(n_side, a_ref, b_ref, *refs):
    o_ref = refs[n_side]
    o_ref[...] = jnp.dot(a_ref[...], b_ref[...], preferred_element_type=F32)
    for w_ref, c_ref in zip(refs[:n_side], refs[n_side + 1:]):
        c_ref[...] = w_ref[...].astype(c_ref.dtype)


def matmul(a, b, side_casts=(), layer=0):
    t, k = a.shape
    n = b.shape[1]
    tm, tn = _row_tile(t, MM_ROWS), IN_PROJ_COLS
    gi, gj = t // tm, n // tn
    in_specs = [pl.BlockSpec((tm, k), lambda i, j: (i, 0), pipeline_mode=pl.Buffered(1)),
                pl.BlockSpec((k, tn), lambda i, j: (0, j))]
    out_shape = [jax.ShapeDtypeStruct((t, n), F32)]
    out_specs = [pl.BlockSpec((tm, tn), lambda i, j: (i, j))]
    for w in side_casts:
        src, shape, dst = _side_cast_specs(w, layer, gi, gj)
        in_specs.append(src)
        out_shape.append(shape)
        out_specs.append(dst)
    res = pl.pallas_call(
        functools.partial(_mm_cast_kernel, len(side_casts)),
        out_shape=tuple(out_shape),
        grid=(gi, gj),
        in_specs=in_specs,
        out_specs=tuple(out_specs),
        compiler_params=_params(("arbitrary", "arbitrary")),
        name="matmul",
    )(a, b, *side_casts)
    return res if side_casts else res[0]


def _mm2_res_kernel(n_a, n_cols, a1_ref, a2_ref, b1_ref, b2_ref, g_ref, *refs):
    res_refs, (x_ref, xg_ref, r_ref, ss) = refs[:-4], refs[-4:]
    j = pl.program_id(1)
    acc = jnp.dot(a1_ref[...], b1_ref[...], preferred_element_type=F32)
    acc = acc + jnp.dot(a2_ref[...], b2_ref[...], preferred_element_type=F32)
    if len(res_refs) == 1:
        x = acc + res_refs[0][...]
    else:
        x = acc + jnp.where(pl.program_id(0) < n_a, res_refs[0][...], res_refs[1][...])
    x_ref[...] = x
    xg_ref[...] = (x * g_ref[...]).astype(xg_ref.dtype)
    sq = x * x
    part = sq[:, :LANES]
    for s in range(1, sq.shape[1] // LANES):
        part = part + sq[:, s * LANES:(s + 1) * LANES]

    @pl.when(j == 0)
    def _():
        ss[...] = part

    @pl.when(j > 0)
    def _():
        ss[...] += part

    @pl.when(j == pl.num_programs(1) - 1)
    def _():
        total = jnp.sum(ss[...], axis=-1, keepdims=True)
        r_ref[...] = jnp.broadcast_to(lax.rsqrt(total * (1.0 / n_cols) + EPS), r_ref.shape)


def matmul2_residual_norm(a1, a2, b, gain, res, res_b=None):
    t, kh = a1.shape
    n = b.shape[1]
    tm = _row_tile(t, MM_ROWS) if res_b is None else _common_tile(MM_ROWS, res.shape[0], t)
    tn = MM_COLS
    specs = [
        pl.BlockSpec((tm, kh), lambda i, j: (i, 0)),
        pl.BlockSpec((tm, kh), lambda i, j: (i, 0)),
        pl.BlockSpec((kh, tn), lambda i, j: (0, j)),
        pl.BlockSpec((kh, tn), lambda i, j: (1, j)),
        pl.BlockSpec((1, tn), lambda i, j: (0, j)),
    ]
    tile = pl.BlockSpec((tm, tn), lambda i, j: (i, j))
    if res_b is None:
        n_a = 0
        specs.append(tile)
        args = (a1, a2, b, b, gain.reshape(1, n), res)
    else:
        n_a = res.shape[0] // tm
        specs.append(pl.BlockSpec((tm, tn), lambda i, j: (jnp.minimum(i, n_a - 1), j)))
        specs.append(pl.BlockSpec((tm, tn), lambda i, j: (jnp.maximum(i - n_a, 0), j)))
        args = (a1, a2, b, b, gain.reshape(1, n), res, res_b)
    return pl.pallas_call(
        functools.partial(_mm2_res_kernel, n_a, n),
        out_shape=(jax.ShapeDtypeStruct((t, n), F32), jax.ShapeDtypeStruct((t, n), BF16),
                   jax.ShapeDtypeStruct((t, LANES), F32)),
        grid=(t // tm, n // tn),
        in_specs=specs,
        out_specs=(tile, tile, pl.BlockSpec((tm, LANES), lambda i, j: (i, 0))),
        scratch_shapes=[pltpu.VMEM((tm, LANES), F32)],
        compiler_params=_params(("parallel", "arbitrary")),
        name="matmul2_residual_norm",
    )(*args)


def _hgrn_masks(cb, width, reverse):
    ti = lax.broadcasted_iota(jnp.int32, (cb, width), 0)
    ji = lax.broadcasted_iota(jnp.int32, (cb, width), 1) % HEAD_DIM
    same = (ti // HGRN_SUB) == (ji // HGRN_SUB)
    masks = [jnp.logical_and(same, (ji >= ti) if reverse else (ji <= ti))]
    hs = HGRN_SUB
    while 2 * hs <= cb:
        grp = (ti // (2 * hs)) == (ji // (2 * hs))
        t_hi = (ti % (2 * hs)) >= hs
        j_hi = (ji % (2 * hs)) >= hs
        if reverse:
            sel = jnp.logical_and(jnp.logical_not(t_hi), j_hi)
        else:
            sel = jnp.logical_and(t_hi, jnp.logical_not(j_hi))
        masks.append(jnp.logical_and(grp, sel))
        hs *= 2
    t1 = lax.broadcasted_iota(jnp.int32, (cb, cb), 0)
    j1 = lax.broadcasted_iota(jnp.int32, (cb, cb), 1)
    tri = ((j1 >= t1) if reverse else (j1 <= t1)).astype(BF16)
    return masks, tri


def _ref_rows(b, group, idx):
    cb, w = b.shape
    g = b.reshape(cb // group, group, w)
    r = jnp.broadcast_to(g[:, idx:idx + 1, :], g.shape)
    return r.reshape(cb, w)


def _cumsum_rows(tri, a):
    w = a.shape[1]
    a1 = a.astype(BF16)
    r1 = a - a1.astype(F32)
    a2 = r1.astype(BF16)
    a3 = (r1 - a2.astype(F32)).astype(BF16)
    out = jnp.dot(tri, jnp.concatenate([a1, a2, a3], axis=1), preferred_element_type=F32)
    return out[:, :w] + out[:, w:2 * w] + out[:, 2 * w:]


def _heads(x):
    return [x[:, h * HEAD_DIM:(h + 1) * HEAD_DIM] for h in range(x.shape[1] // HEAD_DIM)]


def _hgrn_group(qz, fz, v, lb, st_ref, first_head, masks, tri, reverse):
    cb = qz.shape[0]
    q = qz * _sigmoid(qz)
    f = lb + (1.0 - lb) * _sigmoid(fz)
    a = jnp.log(f) * LOG2_E
    k = 1.0 - f
    b = _cumsum_rows(tri, a)
    b_tot = b[0:1, :] if reverse else b[cb - 1:cb, :]

    def scores(eq, ek):
        qs, ks = _heads((q * eq).astype(BF16)), _heads((k * ek).astype(BF16))
        return jnp.concatenate([_dot_nt(qh, kh) for qh, kh in zip(qs, ks)], axis=1)

    r0 = _ref_rows(b, HGRN_SUB, HGRN_SUB // 2 if reverse else HGRN_SUB // 2 - 1)
    att = jnp.where(masks[0], scores(jnp.exp2(b - r0), jnp.exp2(r0 - b)), 0.0)
    hs = HGRN_SUB
    for mask in masks[1:]:
        e = jnp.exp2(-jnp.abs(b - _ref_rows(b, 2 * hs, hs if reverse else hs - 1)))
        att = jnp.where(mask, scores(e, e), att)
        hs *= 2
    att_h = _heads(att.astype(BF16))
    v_h = _heads(v)
    q_dec = _heads((q * jnp.exp2(b)).astype(BF16))
    k_dec = _heads((k * jnp.exp2(b_tot - b)).astype(BF16))
    decay = _heads(jnp.exp2(b_tot))
    outs = []
    for h in range(len(v_h)):
        st = st_ref[first_head + h]
        o = jnp.dot(att_h[h], v_h[h].astype(BF16), preferred_element_type=F32)
        outs.append(o + _dot_nt(q_dec[h], st.astype(BF16)))
        st_ref[first_head + h] = st * decay[h] + jnp.dot(v_h[h].T.astype(BF16), k_dec[h],
                                                         preferred_element_type=F32)
    return jnp.concatenate(outs, axis=1)


def _lower_bound(lbl, layer):
    if layer == 0:
        return jnp.zeros((1, lbl.shape[1]), F32)
    m = jnp.max(lbl, axis=0, keepdims=True)
    e = jnp.exp(lbl - m)
    sm = e / jnp.sum(e, axis=0, keepdims=True)
    return jnp.sum(sm[1:layer + 1, :], axis=0, keepdims=True)


def _hgrn_fwd_kernel(layer, q_ref, f_ref, i_ref, lbl_ref, o_ref, st_ref):
    @pl.when(pl.program_id(1) == 0)
    def _():
        st_ref[...] = jnp.zeros_like(st_ref)

    gw = HGRN_GROUP * HEAD_DIM
    masks, tri = _hgrn_masks(q_ref.shape[0], gw, False)
    lb = _lower_bound(lbl_ref[...], layer)
    for g in range(N_HGRN_HEADS // HGRN_GROUP):
        sl = slice(g * gw, (g + 1) * gw)
        o_ref[:, sl] = _hgrn_group(q_ref[:, sl], f_ref[:, sl], i_ref[:, sl], lb[:, sl], st_ref,
                                   g * HGRN_GROUP, masks, tri, False)


def _hgrn_bwd_kernel(layer, q_ref, f_ref, i_ref, g_ref, of_ref, lbl_ref, nw_ref, o_ref, st_ref):
    @pl.when(pl.program_id(1) == 0)
    def _():
        st_ref[...] = jnp.zeros_like(st_ref)

    gw = HGRN_GROUP * HEAD_DIM
    masks, tri = _hgrn_masks(q_ref.shape[0], gw, True)
    lb = _lower_bound(lbl_ref[...], layer)
    for g in range(N_HGRN_HEADS // HGRN_GROUP):
        sl = slice(g * gw, (g + 1) * gw)
        o = _hgrn_group(q_ref[:, sl], f_ref[:, sl], i_ref[:, sl], lb[:, sl], st_ref,
                        g * HGRN_GROUP, masks, tri, True) + of_ref[:, sl]
        o = jnp.concatenate([oh * lax.rsqrt(jnp.mean(oh * oh, axis=-1, keepdims=True) + EPS)
                             for oh in _heads(o)], axis=1) * nw_ref[:, sl]
        gz = g_ref[:, sl]
        o_ref[:, sl] = (o * (gz * _sigmoid(gz))).astype(o_ref.dtype)


def hgrn2(p, lb_logits_t, norm_w, layer, n_seq, seq_len):
    t = p.shape[0]
    w = N_HGRN_HEADS * HEAD_DIM
    cb = HGRN_BLOCK
    nb = seq_len // cb
    depth = lb_logits_t.shape[1]

    def col(j, reverse):
        if reverse:
            return pl.BlockSpec((cb, w), lambda s, c: (s * nb + nb - 1 - c, j))
        return pl.BlockSpec((cb, w), lambda s, c: (s * nb + c, j))

    def lbl(d):
        return pl.BlockSpec((None, depth, w), lambda s, c: (d, 0, 0))

    scratch = [pltpu.VMEM((N_HGRN_HEADS, HEAD_DIM, HEAD_DIM), F32)]
    o_f = pl.pallas_call(
        functools.partial(_hgrn_fwd_kernel, layer),
        out_shape=jax.ShapeDtypeStruct((t, w), F32),
        grid=(n_seq, nb),
        in_specs=[col(0, False), col(1, False), col(3, False), lbl(0)],
        out_specs=col(0, False),
        scratch_shapes=scratch,
        compiler_params=_params(("parallel", "arbitrary")),
        name="hgrn_fwd",
    )(p, p, p, lb_logits_t)
    return pl.pallas_call(
        functools.partial(_hgrn_bwd_kernel, layer),
        out_shape=jax.ShapeDtypeStruct((t, w), BF16),
        grid=(n_seq, nb),
        in_specs=[col(0, True), col(2, True), col(3, True), col(4, True), col(0, True), lbl(1),
                  pl.BlockSpec((1, w), lambda s, c: (0, 0))],
        out_specs=col(0, True),
        scratch_shapes=scratch,
        compiler_params=_params(("parallel", "arbitrary")),
        name="hgrn_bwd",
    )(p, p, p, p, o_f, lb_logits_t, norm_w.reshape(1, w))


def _rope(x, cos, sin_signed):
    return x * cos + pltpu.roll(x, HEAD_DIM // 2, axis=1) * sin_signed


def _attn_kernel(q_ref, kp_ref, kc_ref, kn_ref, vp_ref, vc_ref, vn_ref,
                 cp_ref, sp_ref, cc_ref, sc_ref, cn_ref, sn_ref, sink_ref, nw_ref, o_ref):
    n = pl.program_id(1)
    nb = pl.num_programs(1)
    tb = q_ref.shape[0]
    rows = GQA_REP * tb
    ri = lax.broadcasted_iota(jnp.int32, (rows, tb), 0) % tb
    ci = lax.broadcasted_iota(jnp.int32, (rows, tb), 1)
    head_of_row = lax.broadcasted_iota(jnp.int32, (rows, 1), 0) // tb
    m_prev = jnp.logical_and(ci >= ri, n > 0)
    m_next = jnp.logical_and(ci <= ri, n < nb - 1)
    scale = HEAD_DIM ** -0.5
    cc, sc = cc_ref[...], sc_ref[...]
    for g in range(N_KV_HEADS):
        sl = slice(g * HEAD_DIM, (g + 1) * HEAD_DIM)
        kcat = jnp.concatenate([
            _rope(kp_ref[:, sl], cp_ref[...], sp_ref[...]),
            _rope(kc_ref[:, sl], cc, sc),
            _rope(kn_ref[:, sl], cn_ref[...], sn_ref[...]),
        ], axis=0).astype(BF16)
        vcat = jnp.concatenate([vp_ref[:, sl], vc_ref[:, sl], vn_ref[:, sl]], axis=0).astype(BF16)
        heads = [g * GQA_REP + r for r in range(GQA_REP)]
        q = jnp.concatenate([_rope(q_ref[:, h * HEAD_DIM:(h + 1) * HEAD_DIM], cc, sc) for h in heads],
                            axis=0).astype(BF16)
        s = _dot_nt(q, kcat) * scale
        s = jnp.concatenate([jnp.where(m_prev, s[:, :tb], NEG_INF), s[:, tb:2 * tb],
                             jnp.where(m_next, s[:, 2 * tb:], NEG_INF)], axis=1)
        sink = jnp.zeros((rows, 1), F32)
        for r, h in enumerate(heads):
            sink = jnp.where(head_of_row == r, sink_ref[h], sink)
        m = jnp.maximum(jnp.max(s, axis=-1, keepdims=True), sink)
        e = jnp.exp(s - m)
        den = jnp.sum(e, axis=-1, keepdims=True) + jnp.exp(sink - m)
        o = jnp.dot(e.astype(BF16), vcat, preferred_element_type=F32) / den
        o = o * lax.rsqrt(jnp.mean(o * o, axis=-1, keepdims=True) + EPS)
        for r, h in enumerate(heads):
            hs = slice(h * HEAD_DIM, (h + 1) * HEAD_DIM)
            o_ref[:, hs] = (o[r * tb:(r + 1) * tb, :] * nw_ref[:, hs]).astype(o_ref.dtype)


def window_attention(p, cos, sin_signed, sink, norm_w, n_seq, seq_len, q_col, k_col, v_col):
    t = p.shape[0]
    tb = ATTN_BLOCK
    nb = seq_len // tb
    qw = N_ATTN_HEADS * HEAD_DIM
    kw = N_KV_HEADS * HEAD_DIM

    def prev(n):
        return jnp.maximum(n - 1, 0)

    def nxt(n):
        return jnp.minimum(n + 1, nb - 1)

    def same(n):
        return n

    def kv(colblk, f):
        return pl.BlockSpec((tb, kw), lambda s, n: (s * nb + f(n), colblk))

    def tab(f):
        return pl.BlockSpec((tb, HEAD_DIM), lambda s, n: (f(n), 0))

    qspec = pl.BlockSpec((tb, qw), lambda s, n: (s * nb + n, q_col // qw))
    kb, vb = k_col // kw, v_col // kw
    return pl.pallas_call(
        _attn_kernel,
        out_shape=jax.ShapeDtypeStruct((t, qw), BF16),
        grid=(n_seq, nb),
        in_specs=[qspec, kv(kb, prev), kv(kb, same), kv(kb, nxt), kv(vb, prev), kv(vb, same), kv(vb, nxt),
                  tab(prev), tab(prev), tab(same), tab(same), tab(nxt), tab(nxt),
                  pl.BlockSpec(memory_space=pltpu.SMEM),
                  pl.BlockSpec((1, qw), lambda s, n: (0, 0))],
        out_specs=pl.BlockSpec((tb, qw), lambda s, n: (s * nb + n, 0)),
        compiler_params=_params(("parallel", "parallel")),
        name="window_attention",
    )(p, p, p, p, p, p, p, cos, sin_signed, cos, sin_signed, cos, sin_signed, sink, norm_w.reshape(1, qw))


def rope_tables(seq_len):
    inv_freq = ROPE_THETA ** (-jnp.arange(0, HEAD_DIM, 2, dtype=F32) / HEAD_DIM)
    ang = jnp.arange(seq_len, dtype=F32)[:, None] * inv_freq[None, :]
    ang = jnp.concatenate([ang, ang], axis=-1)
    sign = jnp.concatenate([-jnp.ones((HEAD_DIM // 2,), F32), jnp.ones((HEAD_DIM // 2,), F32)])
    return jnp.cos(ang), jnp.sin(ang) * sign[None, :]


assert PEER_TOPK == 2 * SUBLANES
_PAIR_LIMIT = [PEER_TOPK // (a + 1) for a in range(PEER_TOPK)]
_N_MID = PEER_TOPK // 2 - 1
_CAND_ROWS = PEER_TOPK + SUBLANES * _N_MID + SUBLANES
assert all(lim <= SUBLANES for lim in _PAIR_LIMIT[1:]) and all(lim == 1 for lim in _PAIR_LIMIT[_N_MID + 1:])


def _extract_top(s, iota, k_out):
    n = s.shape[0]
    vals, idxs = [], []
    for _ in range(k_out):
        m = jnp.max(s, axis=0, keepdims=True)
        idx = jnp.min(jnp.where(s == m, iota, float(n)), axis=0, keepdims=True)
        s = jnp.where(iota == idx, -jnp.inf, s)
        vals.append(m)
        idxs.append(idx)
    return vals, idxs


def _query_route_kernel(x_ref, r_ref, wq_ref, sk_ref, i_ref, j_ref, g_ref, qy_a, qy_b, ti, tj, tg):
    tm = x_ref.shape[0]
    step = pl.program_id(0)
    slabs = wq_ref.shape[2] // HEAD_DIM
    heads_per_trip = slabs // 2
    kio = lax.broadcasted_iota(jnp.int32, (N_KEYS, LANES), 0).astype(F32)
    pio = lax.broadcasted_iota(jnp.int32, (_CAND_ROWS, LANES), 0).astype(F32)
    sub = lax.broadcasted_iota(jnp.int32, (SUBLANES, LANES), 0)

    @pl.when(step == 0)
    def _():
        qy_b[...] = jnp.zeros_like(qy_b)

    halves = tm // LANES
    assert heads_per_trip * halves == slabs, "one projection slab per (head, token half) unit"

    def unit_scores(h, half, q_read):
        tok = slice(half * LANES, (half + 1) * LANES)
        return [_dot_nt(sk_ref[h, c].astype(BF16), q_read[2 * h + c, tok, :].astype(BF16)) for c in range(2)]

    def unit_route(h, half, scores):
        row = pl.multiple_of(h * PEER_TOPK, PEER_TOPK)
        tok = slice(half * LANES, (half + 1) * LANES)
        (v0, i0), (v1, i1) = [_extract_top(s, kio, PEER_TOPK) for s in scores]
        v1_lo = jnp.concatenate(v1[:SUBLANES], axis=0)
        groups = [v0[0] + jnp.concatenate(v1, axis=0)]
        for a in range(1, _N_MID + 1):
            groups.append(jnp.where(sub < _PAIR_LIMIT[a], v0[a] + v1_lo, -jnp.inf))
        groups.append(jnp.concatenate(v0[_N_MID + 1:], axis=0) + v1[0])
        top_v, pos = _extract_top(jnp.concatenate(groups, axis=0), pio, PEER_TOPK)
        top_v, pos = jnp.concatenate(top_v, axis=0), jnp.concatenate(pos, axis=0)
        mid = jnp.floor((pos - PEER_TOPK) * (1.0 / SUBLANES))
        last = float(PEER_TOPK + SUBLANES * _N_MID)
        a_sel = jnp.where(pos < PEER_TOPK, 0.0, jnp.where(pos < last, 1.0 + mid, pos - (last - _N_MID - 1)))
        b_sel = jnp.where(pos < PEER_TOPK, pos,
                          jnp.where(pos < last, pos - PEER_TOPK - SUBLANES * mid, 0.0))
        ki = jnp.zeros_like(pos)
        kj = jnp.zeros_like(pos)
        for a in range(PEER_TOPK):
            ki = ki + jnp.where(a_sel == float(a), i0[a], 0.0)
            kj = kj + jnp.where(b_sel == float(a), i1[a], 0.0)
        e = jnp.exp(top_v - top_v[0:1, :])
        ti[pl.ds(row, PEER_TOPK), tok] = ki
        tj[pl.ds(row, PEER_TOPK), tok] = kj
        tg[pl.ds(row, PEER_TOPK), tok] = e / jnp.sum(e, axis=0, keepdims=True)

    def run(q_write, q_read):
        def trip(g, carry):
            units = [(g * heads_per_trip + u, half) for u in range(heads_per_trip) for half in range(halves)]
            x = x_ref[...]
            r = r_ref[...]
            scores = []
            for s, (h, half) in enumerate(units):
                scores.append(unit_scores(h, half, q_read))
                if s % 2 == 1:
                    acc = jnp.dot(x, wq_ref[g, :, (s - 1) * HEAD_DIM:(s + 1) * HEAD_DIM],
                                  preferred_element_type=F32)
                    q_write[g * slabs + s - 1] = acc[:, :HEAD_DIM] * r
                    q_write[g * slabs + s] = acc[:, HEAD_DIM:] * r
            for (h, half), sc in zip(units, scores):
                unit_route(h, half, sc)
            return carry

        lax.fori_loop(0, wq_ref.shape[0], trip, 0)

    @pl.when(lax.rem(step, 2) == 0)
    def _():
        run(qy_a, qy_b)

    @pl.when(lax.rem(step, 2) == 1)
    def _():
        run(qy_b, qy_a)

    i_ref[...] = ti[...].T
    j_ref[...] = tj[...].T
    g_ref[...] = tg[...].T


def peer_query_route(x, r, wq, sub_keys):
    t, d = x.shape
    tm = _row_tile(t, ROUTE_TILE)
    nt = t // tm
    slots = PEER_HEADS * PEER_TOPK
    out = jax.ShapeDtypeStruct((t, slots), F32)
    ospec = pl.BlockSpec((tm, slots), lambda s: (jnp.maximum(s - 1, 0), 0))
    full = pltpu.VMEM((slots, tm), F32)
    queries = pltpu.VMEM((2 * PEER_HEADS, tm, HEAD_DIM), F32)
    return pl.pallas_call(
        _query_route_kernel,
        out_shape=(out, out, out),
        grid=(nt + 1,),
        in_specs=[pl.BlockSpec((tm, d), lambda s: (jnp.minimum(s, nt - 1), 0)),
                  pl.BlockSpec((tm, LANES), lambda s: (jnp.minimum(s, nt - 1), 0)),
                  pl.BlockSpec(wq.shape, lambda s: (0, 0, 0), pipeline_mode=pl.Buffered(1)),
                  pl.BlockSpec(sub_keys.shape, lambda s: (0, 0, 0, 0))],
        out_specs=(ospec, ospec, ospec),
        scratch_shapes=[queries, queries, full, full, full],
        compiler_params=_params(("arbitrary",)),
        name="peer_query_route",
    )(x, r, wq, sub_keys)


def _coef_kernel(i_ref, j_ref, g_ref, c_ref, stage):
    tm = i_ref.shape[0]
    sub = lax.broadcasted_iota(jnp.int32, (N_KEYS, i_ref.shape[1]), 0).astype(F32)

    def group(t0, buf):
        ib = i_ref[pl.ds(t0, COEF_GROUP), :]
        jb = j_ref[pl.ds(t0, COEF_GROUP), :]
        gb = g_ref[pl.ds(t0, COEF_GROUP), :]
        for t in range(COEF_GROUP):
            ptg = jnp.where(sub == ib[t:t + 1, :], gb[t:t + 1, :], 0.0).astype(BF16)
            qt = jnp.where(sub == jb[t:t + 1, :], 1.0, 0.0).astype(BF16)
            buf[t * COEF_PITCH:t * COEF_PITCH + N_KEYS, :] = _dot_nt(ptg, qt)
        for i in range(N_KEYS):
            c_ref[i, pl.ds(t0, COEF_GROUP), :] = buf[pl.ds(i, COEF_GROUP, stride=COEF_PITCH), :].astype(c_ref.dtype)

    def trip(gi, carry):
        for u in range(COEF_UNROLL):
            group(pl.multiple_of((gi * COEF_UNROLL + u) * COEF_GROUP, COEF_GROUP), stage.at[u])
        return carry

    lax.fori_loop(0, tm // (COEF_GROUP * COEF_UNROLL), trip, 0)


def peer_coefficients(ki, kj, gate):
    t, slots = ki.shape
    tm = _row_tile(t, COEF_TILE)
    spec = pl.BlockSpec((tm, slots), lambda i: (i, 0))
    return pl.pallas_call(
        _coef_kernel,
        out_shape=jax.ShapeDtypeStruct((N_KEYS, t, N_KEYS), BF16),
        grid=(t // tm,),
        in_specs=[spec, spec, spec],
        out_specs=pl.BlockSpec((N_KEYS, tm, N_KEYS), lambda i: (0, i, 0)),
        scratch_shapes=[pltpu.VMEM((COEF_UNROLL, COEF_GROUP * COEF_PITCH, N_KEYS), F32)],
        compiler_params=_params(("parallel",)),
        name="peer_coefficients",
    )(ki, kj, gate)


def _gelu(x):
    return 0.5 * x * (1.0 + lax.erf(x * 0.7071067811865476))


def _peer_kernel(n_side, x_ref, r_ref, u_ref, v_ref, c_ref, *refs):
    o_ref = refs[n_side]

    @pl.when(pl.program_id(1) == 0)
    def _():
        o_ref[...] = jnp.zeros_like(o_ref)

    hid = _dot_nt(x_ref[...], u_ref[...])
    r = r_ref[...]
    w = jnp.concatenate([(c_ref[s].astype(F32) * _gelu(hid[:, s * N_KEYS:(s + 1) * N_KEYS] * r)).astype(BF16)
                         for s in range(c_ref.shape[0])], axis=1)
    o_ref[...] += jnp.dot(w, v_ref[...], preferred_element_type=F32)
    for w_ref, cast_ref in zip(refs[:n_side], refs[n_side + 1:]):
        cast_ref[...] = w_ref[...].astype(cast_ref.dtype)


def _side_cast_specs(w, layer, gi, gj):
    _, r, c = w.shape
    rows = BF16_ROWS
    while rows * gi * gj < r:
        rows *= 2
    assert r % rows == 0
    last = r // rows - 1
    src = pl.BlockSpec((None, rows, c), lambda i, j: (layer, jnp.minimum(i * gj + j, last), 0))
    dst = pl.BlockSpec((rows, c), lambda i, j: (jnp.minimum(i * gj + j, last), 0))
    return src, jax.ShapeDtypeStruct((r, c), BF16), dst


def peer_dense(x, r, u, v, coef, side_casts=(), layer=0):
    t, d = x.shape
    e = u.shape[0]
    tm, te = _row_tile(t, PEER_ROWS), PEER_EXPERTS
    gi, gj = t // tm, e // te
    in_specs = [pl.BlockSpec((tm, d), lambda i, j: (i, 0), pipeline_mode=pl.Buffered(1)),
                pl.BlockSpec((tm, LANES), lambda i, j: (i, 0)),
                pl.BlockSpec((te, d), lambda i, j: (j, 0)),
                pl.BlockSpec((te, d), lambda i, j: (j, 0)),
                pl.BlockSpec((te // N_KEYS, tm, N_KEYS), lambda i, j: (j, i, 0))]
    out_shape = [jax.ShapeDtypeStruct((t, d), F32)]
    out_specs = [pl.BlockSpec((tm, d), lambda i, j: (i, 0), pipeline_mode=pl.Buffered(1))]
    for w in side_casts:
        src, shape, dst = _side_cast_specs(w, layer, gi, gj)
        in_specs.append(src)
        out_shape.append(shape)
        out_specs.append(dst)
    res = pl.pallas_call(
        functools.partial(_peer_kernel, len(side_casts)),
        out_shape=tuple(out_shape),
        grid=(gi, gj),
        in_specs=in_specs,
        out_specs=tuple(out_specs),
        compiler_params=_params(("arbitrary", "arbitrary")),
        name="peer_dense",
    )(x, r, u, v, coef, *side_casts)
    return res if side_casts else res[0]


def kernel(x_prompt, x_sample, w_in, w_out, norm_mix, norm_ffn, lb_logits, hgrn_norm, attn_norm, attn_sink,
           peer_query, peer_sub_keys, peer_u, peer_v, final_norm):
    depth = w_in.shape[0]
    d_model = x_prompt.shape[-1]
    seq_len = x_prompt.shape[1]
    assert x_sample.shape[1] == seq_len, "prompt and sample sequences are stacked and must share a length"
    n_seq = x_prompt.shape[0] + x_sample.shape[0]
    xp = x_prompt.reshape(-1, d_model)
    xs = x_sample.reshape(-1, d_model)
    t_prompt, t_sample = xp.shape[0], xs.shape[0]
    hw = N_HGRN_HEADS * HEAD_DIM
    q_col = 5 * hw
    k_col = q_col + N_ATTN_HEADS * HEAD_DIM
    v_col = k_col + N_KV_HEADS * HEAD_DIM

    cos, sin_signed = rope_tables(seq_len)
    lbl_t = jnp.transpose(lb_logits.astype(F32), (1, 0, 2))

    h = rmsnorm_rows2(xp, xs, norm_mix[0], BF16)
    w_in_bf16 = cast_bf16(w_in, 0)
    x = None
    for l in range(depth):
        p, u_bf16, v_bf16, w_out_bf16 = matmul(h, w_in_bf16, side_casts=(peer_u, peer_v, w_out), layer=l)
        o_h = hgrn2(p, lbl_t, hgrn_norm[l], l, n_seq, seq_len)
        o_a = window_attention(p, cos, sin_signed, attn_sink[l].astype(F32), attn_norm[l], n_seq, seq_len,
                               q_col, k_col, v_col)
        if l == 0:
            x, xg, r = matmul2_residual_norm(o_h, o_a, w_out_bf16, norm_ffn[l], xp, xs)
        else:
            x, xg, r = matmul2_residual_norm(o_h, o_a, w_out_bf16, norm_ffn[l], x)
        ki, kj, gate = peer_query_route(xg, r, cast_bf16(peer_query, l, col_panels=True), peer_sub_keys[l])
        coef = peer_coefficients(ki, kj, gate)
        if l + 1 < depth:
            po, w_in_bf16 = peer_dense(xg, r, u_bf16, v_bf16, coef, side_casts=(w_in,), layer=l + 1)
            x, h = add_rmsnorm_rows(x, po, norm_mix[l + 1], BF16)
        else:
            po = peer_dense(xg, r, u_bf16, v_bf16, coef)
    y_prompt = add_rmsnorm_final(x, po, final_norm, 0, t_prompt)
    y_sample = add_rmsnorm_final(x, po, final_norm, t_prompt, t_sample)
    return (y_prompt.reshape(x_prompt.shape), y_sample.reshape(x_sample.shape))
```

```python
import functools

import jax
import jax.numpy as jnp
from jax import lax
from jax.experimental import pallas as pl
from jax.experimental.pallas import tpu as pltpu

F32 = jnp.float32
BF16 = jnp.bfloat16

LANES = 128
SUBLANES = 8
BF16_ROWS = 2 * SUBLANES
HEAD_DIM = LANES
N_HGRN_HEADS = 16
N_ATTN_HEADS = 16
N_KV_HEADS = 4
GQA_REP = N_ATTN_HEADS // N_KV_HEADS
WINDOW = 128
ROPE_THETA = 10000.0
HGRN_SUB = 16
N_KEYS = 128
PEER_HEADS = 8
PEER_TOPK = 16
EPS = 1e-6
NEG_INF = -1e30
LOG2_E = 1.4426950408889634

VMEM_LIMIT = 56 * 1024 * 1024

NORM_ROWS = 256
MM_ROWS = 1024
MM_COLS = 512
CAST_BLOCK = 1024
HGRN_BLOCK = 128
HGRN_GROUP = 8
ATTN_BLOCK = WINDOW
ROUTE_TILE = 256
COEF_TILE = 256
COEF_GROUP = 16
COEF_PITCH = N_KEYS + SUBLANES
COEF_UNROLL = 8
PEER_ROWS = 1024
PEER_EXPERTS = 512


def _params(sem):
    return pltpu.CompilerParams(dimension_semantics=sem, vmem_limit_bytes=VMEM_LIMIT)


def _sigmoid(x):
    return 1.0 / (1.0 + jnp.exp(-x))


def _dot_nt(a, b):
    return lax.dot_general(a, b, (((1,), (1,)), ((), ())), preferred_element_type=F32)


def _row_tile(t, want):
    return want if t % want == 0 else t


def _common_tile(want, *sizes):
    while any(n % want for n in sizes):
        want //= 2
    return want


def _rms(x, g):
    return x * lax.rsqrt(jnp.mean(x * x, axis=-1, keepdims=True) + EPS) * g


def _cast_kernel(w_ref, o_ref):
    o_ref[...] = w_ref[...].astype(o_ref.dtype)


def cast_bf16(w, layer, col_panels=False):
    _, r, c = w.shape
    br, bc = _row_tile(r, CAST_BLOCK), _row_tile(c, CAST_BLOCK)
    if col_panels:
        out_shape = jax.ShapeDtypeStruct((c // bc, r, bc), BF16)
        out_spec = pl.BlockSpec((None, br, bc), lambda i, j: (j, i, 0))
    else:
        out_shape = jax.ShapeDtypeStruct((r, c), BF16)
        out_spec = pl.BlockSpec((br, bc), lambda i, j: (i, j))
    return pl.pallas_call(
        _cast_kernel,
        out_shape=out_shape,
        grid=(r // br, c // bc),
        in_specs=[pl.BlockSpec((None, br, bc), lambda i, j: (layer, i, j))],
        out_specs=out_spec,
        compiler_params=_params(("parallel", "parallel")),
        name="cast_bf16",
    )(w)


def _norm2_kernel(n_a, xa_ref, xb_ref, g_ref, h_ref):
    x = jnp.where(pl.program_id(0) < n_a, xa_ref[...], xb_ref[...])
    h_ref[...] = _rms(x, g_ref[...]).astype(h_ref.dtype)


def _norm_kernel(x_ref, g_ref, h_ref):
    h_ref[...] = _rms(x_ref[...], g_ref[...]).astype(h_ref.dtype)


def _addnorm_kernel(x_ref, y_ref, g_ref, s_ref, h_ref):
    x = x_ref[...] + y_ref[...]
    s_ref[...] = x
    h_ref[...] = _rms(x, g_ref[...]).astype(h_ref.dtype)


def _addnorm_final_kernel(x_ref, y_ref, g_ref, h_ref):
    h_ref[...] = _rms(x_ref[...] + y_ref[...], g_ref[...]).astype(h_ref.dtype)


def _two_source_specs(ta, tm, width):
    n_a = ta // tm
    spec_a = pl.BlockSpec((tm, width), lambda i, *_: (jnp.minimum(i, n_a - 1), 0))
    spec_b = pl.BlockSpec((tm, width), lambda i, *_: (jnp.maximum(i - n_a, 0), 0))
    return n_a, spec_a, spec_b


def rmsnorm_rows2(xa, xb, g, out_dtype):
    ta, d = xa.shape
    t = ta + xb.shape[0]
    tm = _common_tile(NORM_ROWS, ta, t)
    n_a, spec_a, spec_b = _two_source_specs(ta, tm, d)
    return pl.pallas_call(
        functools.partial(_norm2_kernel, n_a),
        out_shape=jax.ShapeDtypeStruct((t, d), out_dtype),
        grid=(t // tm,),
        in_specs=[spec_a, spec_b, pl.BlockSpec((1, d), lambda i: (0, 0))],
        out_specs=pl.BlockSpec((tm, d), lambda i: (i, 0)),
        compiler_params=_params(("parallel",)),
        name="rmsnorm2",
    )(xa, xb, g.reshape(1, d))


def rmsnorm_rows(x, g, out_dtype):
    t, d = x.shape
    tm = _row_tile(t, NORM_ROWS)
    row = pl.BlockSpec((tm, d), lambda i: (i, 0))
    return pl.pallas_call(
        _norm_kernel,
        out_shape=jax.ShapeDtypeStruct((t, d), out_dtype),
        grid=(t // tm,),
        in_specs=[row, pl.BlockSpec((1, d), lambda i: (0, 0))],
        out_specs=row,
        compiler_params=_params(("parallel",)),
        name="rmsnorm",
    )(x, g.reshape(1, d))


def add_rmsnorm_rows(x, y, g, out_dtype):
    t, d = x.shape
    tm = _row_tile(t, NORM_ROWS)
    row = pl.BlockSpec((tm, d), lambda i: (i, 0))
    return pl.pallas_call(
        _addnorm_kernel,
        out_shape=(jax.ShapeDtypeStruct((t, d), F32), jax.ShapeDtypeStruct((t, d), out_dtype)),
        grid=(t // tm,),
        in_specs=[row, row, pl.BlockSpec((1, d), lambda i: (0, 0))],
        out_specs=(row, row),
        compiler_params=_params(("parallel",)),
        name="add_rmsnorm",
    )(x, y, g.reshape(1, d))


def add_rmsnorm_final(x, y, g, row_start, n_rows):
    d = x.shape[1]
    tm = _row_tile(n_rows, NORM_ROWS)
    off = row_start // tm
    src = pl.BlockSpec((tm, d), lambda i: (i + off, 0))
    return pl.pallas_call(
        _addnorm_final_kernel,
        out_shape=jax.ShapeDtypeStruct((n_rows, d), F32),
        grid=(n_rows // tm,),
        in_specs=[src, src, pl.BlockSpec((1, d), lambda i: (0, 0))],
        out_specs=pl.BlockSpec((tm, d), lambda i: (i, 0)),
        compiler_params=_params(("parallel",)),
        name="add_rmsnorm_final",
    )(x, y, g.reshape(1, d))


def _mm_cast_kernel(n_side, a_ref, b_ref, *refs):
    o_ref = refs[n_side]
    o_ref[...] = jnp.dot(a_ref[...], b_ref[...], preferred_element_type=F32)
    for w_ref, c_ref in zip(refs[:n_side], refs[n_side + 1:]):
        c_ref[...] = w_ref[...].astype(c_ref.dtype)


def matmul(a, b, side_casts=(), layer=0):
    t, k = a.shape
    n = b.shape[1]
    tm, tn = _row_tile(t, MM_ROWS), MM_COLS
    gi, gj = t // tm, n // tn
    in_specs = [pl.BlockSpec((tm, k), lambda i, j: (i, 0)), pl.BlockSpec((k, tn), lambda i, j: (0, j))]
    out_shape = [jax.ShapeDtypeStruct((t, n), F32)]
    out_specs = [pl.BlockSpec((tm, tn), lambda i, j: (i, j))]
    for w in side_casts:
        src, shape, dst = _side_cast_specs(w, layer, gi, gj)
        in_specs.append(src)
        out_shape.append(shape)
        out_specs.append(dst)
    res = pl.pallas_call(
        functools.partial(_mm_cast_kernel, len(side_casts)),
        out_shape=tuple(out_shape),
        grid=(gi, gj),
        in_specs=in_specs,
        out_specs=tuple(out_specs),
        compiler_params=_params(("arbitrary", "arbitrary")),
        name="matmul",
    )(a, b, *side_casts)
    return res if side_casts else res[0]


def _mm2_res_kernel(n_a, n_cols, a1_ref, a2_ref, b1_ref, b2_ref, g_ref, *refs):
    res_refs, (x_ref, xg_ref, r_ref, ss) = refs[:-4], refs[-4:]
    j = pl.program_id(1)
    acc = jnp.dot(a1_ref[...], b1_ref[...], preferred_element_type=F32)
    acc = acc + jnp.dot(a2_ref[...], b2_ref[...], preferred_element_type=F32)
    if len(res_refs) == 1:
        x = acc + res_refs[0][...]
    else:
        x = acc + jnp.where(pl.program_id(0) < n_a, res_refs[0][...], res_refs[1][...])
    x_ref[...] = x
    xg_ref[...] = (x * g_ref[...]).astype(xg_ref.dtype)
    sq = x * x
    part = sq[:, :LANES]
    for s in range(1, sq.shape[1] // LANES):
        part = part + sq[:, s * LANES:(s + 1) * LANES]

    @pl.when(j == 0)
    def _():
        ss[...] = part

    @pl.when(j > 0)
    def _():
        ss[...] += part

    @pl.when(j == pl.num_programs(1) - 1)
    def _():
        total = jnp.sum(ss[...], axis=-1, keepdims=True)
        r_ref[...] = jnp.broadcast_to(lax.rsqrt(total * (1.0 / n_cols) + EPS), r_ref.shape)


def matmul2_residual_norm(a1, a2, b, gain, res, res_b=None):
    t, kh = a1.shape
    n = b.shape[1]
    tm = _row_tile(t, MM_ROWS) if res_b is None else _common_tile(MM_ROWS, res.shape[0], t)
    tn = MM_COLS
    specs = [
        pl.BlockSpec((tm, kh), lambda i, j: (i, 0)),
        pl.BlockSpec((tm, kh), lambda i, j: (i, 0)),
        pl.BlockSpec((kh, tn), lambda i, j: (0, j)),
        pl.BlockSpec((kh, tn), lambda i, j: (1, j)),
        pl.BlockSpec((1, tn), lambda i, j: (0, j)),
    ]
    tile = pl.BlockSpec((tm, tn), lambda i, j: (i, j))
    if res_b is None:
        n_a = 0
        specs.append(tile)
        args = (a1, a2, b, b, gain.reshape(1, n), res)
    else:
        n_a = res.shape[0] // tm
        specs.append(pl.BlockSpec((tm, tn), lambda i, j: (jnp.minimum(i, n_a - 1), j)))
        specs.append(pl.BlockSpec((tm, tn), lambda i, j: (jnp.maximum(i - n_a, 0), j)))
        args = (a1, a2, b, b, gain.reshape(1, n), res, res_b)
    return pl.pallas_call(
        functools.partial(_mm2_res_kernel, n_a, n),
        out_shape=(jax.ShapeDtypeStruct((t, n), F32), jax.ShapeDtypeStruct((t, n), BF16),
                   jax.ShapeDtypeStruct((t, LANES), F32)),
        grid=(t // tm, n // tn),
        in_specs=specs,
        out_specs=(tile, tile, pl.BlockSpec((tm, LANES), lambda i, j: (i, 0))),
        scratch_shapes=[pltpu.VMEM((tm, LANES), F32)],
        compiler_params=_params(("parallel", "arbitrary")),
        name="matmul2_residual_norm",
    )(*args)


def _hgrn_masks(cb, width, reverse):
    ti = lax.broadcasted_iota(jnp.int32, (cb, width), 0)
    ji = lax.broadcasted_iota(jnp.int32, (cb, width), 1) % HEAD_DIM
    same = (ti // HGRN_SUB) == (ji // HGRN_SUB)
    masks = [jnp.logical_and(same, (ji >= ti) if reverse else (ji <= ti))]
    hs = HGRN_SUB
    while 2 * hs <= cb:
        grp = (ti // (2 * hs)) == (ji // (2 * hs))
        t_hi = (ti % (2 * hs)) >= hs
        j_hi = (ji % (2 * hs)) >= hs
        if reverse:
            sel = jnp.logical_and(jnp.logical_not(t_hi), j_hi)
        else:
            sel = jnp.logical_and(t_hi, jnp.logical_not(j_hi))
        masks.append(jnp.logical_and(grp, sel))
        hs *= 2
    t1 = lax.broadcasted_iota(jnp.int32, (cb, cb), 0)
    j1 = lax.broadcasted_iota(jnp.int32, (cb, cb), 1)
    tri = ((j1 >= t1) if reverse else (j1 <= t1)).astype(BF16)
    return masks, tri


def _ref_rows(b, group, idx):
    cb, w = b.shape
    g = b.reshape(cb // group, group, w)
    r = jnp.broadcast_to(g[:, idx:idx + 1, :], g.shape)
    return r.reshape(cb, w)


def _cumsum_rows(tri, a):
    w = a.shape[1]
    a1 = a.astype(BF16)
    r1 = a - a1.astype(F32)
    a2 = r1.astype(BF16)
    a3 = (r1 - a2.astype(F32)).astype(BF16)
    out = jnp.dot(tri, jnp.concatenate([a1, a2, a3], axis=1), preferred_element_type=F32)
    return out[:, :w] + out[:, w:2 * w] + out[:, 2 * w:]


def _heads(x):
    return [x[:, h * HEAD_DIM:(h + 1) * HEAD_DIM] for h in range(x.shape[1] // HEAD_DIM)]


def _hgrn_group(qz, fz, v, lb, st_ref, first_head, masks, tri, reverse):
    cb = qz.shape[0]
    q = qz * _sigmoid(qz)
    f = lb + (1.0 - lb) * _sigmoid(fz)
    a = jnp.log(f) * LOG2_E
    k = 1.0 - f
    b = _cumsum_rows(tri, a)
    b_tot = b[0:1, :] if reverse else b[cb - 1:cb, :]

    def scores(eq, ek):
        qs, ks = _heads((q * eq).astype(BF16)), _heads((k * ek).astype(BF16))
        return jnp.concatenate([_dot_nt(qh, kh) for qh, kh in zip(qs, ks)], axis=1)

    r0 = _ref_rows(b, HGRN_SUB, HGRN_SUB // 2 if reverse else HGRN_SUB // 2 - 1)
    att = jnp.where(masks[0], scores(jnp.exp2(b - r0), jnp.exp2(r0 - b)), 0.0)
    hs = HGRN_SUB
    for mask in masks[1:]:
        e = jnp.exp2(-jnp.abs(b - _ref_rows(b, 2 * hs, hs if reverse else hs - 1)))
        att = jnp.where(mask, scores(e, e), att)
        hs *= 2
    att_h = _heads(att.astype(BF16))
    v_h = _heads(v)
    q_dec = _heads((q * jnp.exp2(b)).astype(BF16))
    k_dec = _heads((k * jnp.exp2(b_tot - b)).astype(BF16))
    decay = _heads(jnp.exp2(b_tot))
    outs = []
    for h in range(len(v_h)):
        st = st_ref[first_head + h]
        o = jnp.dot(att_h[h], v_h[h].astype(BF16), preferred_element_type=F32)
        outs.append(o + _dot_nt(q_dec[h], st.astype(BF16)))
        st_ref[first_head + h] = st * decay[h] + jnp.dot(v_h[h].T.astype(BF16), k_dec[h],
                                                         preferred_element_type=F32)
    return jnp.concatenate(outs, axis=1)


def _lower_bound(lbl, layer):
    if layer == 0:
        return jnp.zeros((1, lbl.shape[1]), F32)
    m = jnp.max(lbl, axis=0, keepdims=True)
    e = jnp.exp(lbl - m)
    sm = e / jnp.sum(e, axis=0, keepdims=True)
    return jnp.sum(sm[1:layer + 1, :], axis=0, keepdims=True)


def _hgrn_fwd_kernel(layer, q_ref, f_ref, i_ref, lbl_ref, o_ref, st_ref):
    @pl.when(pl.program_id(1) == 0)
    def _():
        st_ref[...] = jnp.zeros_like(st_ref)

    gw = HGRN_GROUP * HEAD_DIM
    masks, tri = _hgrn_masks(q_ref.shape[0], gw, False)
    lb = _lower_bound(lbl_ref[...], layer)
    for g in range(N_HGRN_HEADS // HGRN_GROUP):
        sl = slice(g * gw, (g + 1) * gw)
        o_ref[:, sl] = _hgrn_group(q_ref[:, sl], f_ref[:, sl], i_ref[:, sl], lb[:, sl], st_ref,
                                   g * HGRN_GROUP, masks, tri, False)


def _hgrn_bwd_kernel(layer, q_ref, f_ref, i_ref, g_ref, of_ref, lbl_ref, nw_ref, o_ref, st_ref):
    @pl.when(pl.program_id(1) == 0)
    def _():
        st_ref[...] = jnp.zeros_like(st_ref)

    gw = HGRN_GROUP * HEAD_DIM
    masks, tri = _hgrn_masks(q_ref.shape[0], gw, True)
    lb = _lower_bound(lbl_ref[...], layer)
    for g in range(N_HGRN_HEADS // HGRN_GROUP):
        sl = slice(g * gw, (g + 1) * gw)
        o = _hgrn_group(q_ref[:, sl], f_ref[:, sl], i_ref[:, sl], lb[:, sl], st_ref,
                        g * HGRN_GROUP, masks, tri, True) + of_ref[:, sl]
        o = jnp.concatenate([oh * lax.rsqrt(jnp.mean(oh * oh, axis=-1, keepdims=True) + EPS)
                             for oh in _heads(o)], axis=1) * nw_ref[:, sl]
        gz = g_ref[:, sl]
        o_ref[:, sl] = (o * (gz * _sigmoid(gz))).astype(o_ref.dtype)


def hgrn2(p, lb_logits_t, norm_w, layer, n_seq, seq_len):
    t = p.shape[0]
    w = N_HGRN_HEADS * HEAD_DIM
    cb = HGRN_BLOCK
    nb = seq_len // cb
    depth = lb_logits_t.shape[1]

    def col(j, reverse):
        if reverse:
            return pl.BlockSpec((cb, w), lambda s, c: (s * nb + nb - 1 - c, j))
        return pl.BlockSpec((cb, w), lambda s, c: (s * nb + c, j))

    def lbl(d):
        return pl.BlockSpec((None, depth, w), lambda s, c: (d, 0, 0))

    scratch = [pltpu.VMEM((N_HGRN_HEADS, HEAD_DIM, HEAD_DIM), F32)]
    o_f = pl.pallas_call(
        functools.partial(_hgrn_fwd_kernel, layer),
        out_shape=jax.ShapeDtypeStruct((t, w), F32),
        grid=(n_seq, nb),
        in_specs=[col(0, False), col(1, False), col(3, False), lbl(0)],
        out_specs=col(0, False),
        scratch_shapes=scratch,
        compiler_params=_params(("parallel", "arbitrary")),
        name="hgrn_fwd",
    )(p, p, p, lb_logits_t)
    return pl.pallas_call(
        functools.partial(_hgrn_bwd_kernel, layer),
        out_shape=jax.ShapeDtypeStruct((t, w), BF16),
        grid=(n_seq, nb),
        in_specs=[col(0, True), col(2, True), col(3, True), col(4, True), col(0, True), lbl(1),
                  pl.BlockSpec((1, w), lambda s, c: (0, 0))],
        out_specs=col(0, True),
        scratch_shapes=scratch,
        compiler_params=_params(("parallel", "arbitrary")),
        name="hgrn_bwd",
    )(p, p, p, p, o_f, lb_logits_t, norm_w.reshape(1, w))


def _rope(x, cos, sin_signed):
    return x * cos + pltpu.roll(x, HEAD_DIM // 2, axis=1) * sin_signed


def _attn_kernel(q_ref, kp_ref, kc_ref, kn_ref, vp_ref, vc_ref, vn_ref,
                 cp_ref, sp_ref, cc_ref, sc_ref, cn_ref, sn_ref, sink_ref, nw_ref, o_ref):
    n = pl.program_id(1)
    nb = pl.num_programs(1)
    tb = q_ref.shape[0]
    rows = GQA_REP * tb
    ri = lax.broadcasted_iota(jnp.int32, (rows, tb), 0) % tb
    ci = lax.broadcasted_iota(jnp.int32, (rows, tb), 1)
    head_of_row = lax.broadcasted_iota(jnp.int32, (rows, 1), 0) // tb
    m_prev = jnp.logical_and(ci >= ri, n > 0)
    m_next = jnp.logical_and(ci <= ri, n < nb - 1)
    scale = HEAD_DIM ** -0.5
    cc, sc = cc_ref[...], sc_ref[...]
    for g in range(N_KV_HEADS):
        sl = slice(g * HEAD_DIM, (g + 1) * HEAD_DIM)
        kcat = jnp.concatenate([
            _rope(kp_ref[:, sl], cp_ref[...], sp_ref[...]),
            _rope(kc_ref[:, sl], cc, sc),
            _rope(kn_ref[:, sl], cn_ref[...], sn_ref[...]),
        ], axis=0).astype(BF16)
        vcat = jnp.concatenate([vp_ref[:, sl], vc_ref[:, sl], vn_ref[:, sl]], axis=0).astype(BF16)
        heads = [g * GQA_REP + r for r in range(GQA_REP)]
        q = jnp.concatenate([_rope(q_ref[:, h * HEAD_DIM:(h + 1) * HEAD_DIM], cc, sc) for h in heads],
                            axis=0).astype(BF16)
        s = _dot_nt(q, kcat) * scale
        s = jnp.concatenate([jnp.where(m_prev, s[:, :tb], NEG_INF), s[:, tb:2 * tb],
                             jnp.where(m_next, s[:, 2 * tb:], NEG_INF)], axis=1)
        sink = jnp.zeros((rows, 1), F32)
        for r, h in enumerate(heads):
            sink = jnp.where(head_of_row == r, sink_ref[h], sink)
        m = jnp.maximum(jnp.max(s, axis=-1, keepdims=True), sink)
        e = jnp.exp(s - m)
        den = jnp.sum(e, axis=-1, keepdims=True) + jnp.exp(sink - m)
        o = jnp.dot(e.astype(BF16), vcat, preferred_element_type=F32) / den
        o = o * lax.rsqrt(jnp.mean(o * o, axis=-1, keepdims=True) + EPS)
        for r, h in enumerate(heads):
            hs = slice(h * HEAD_DIM, (h + 1) * HEAD_DIM)
            o_ref[:, hs] = (o[r * tb:(r + 1) * tb, :] * nw_ref[:, hs]).astype(o_ref.dtype)


def window_attention(p, cos, sin_signed, sink, norm_w, n_seq, seq_len, q_col, k_col, v_col):
    t = p.shape[0]
    tb = ATTN_BLOCK
    nb = seq_len // tb
    qw = N_ATTN_HEADS * HEAD_DIM
    kw = N_KV_HEADS * HEAD_DIM

    def prev(n):
        return jnp.maximum(n - 1, 0)

    def nxt(n):
        return jnp.minimum(n + 1, nb - 1)

    def same(n):
        return n

    def kv(colblk, f):
        return pl.BlockSpec((tb, kw), lambda s, n: (s * nb + f(n), colblk))

    def tab(f):
        return pl.BlockSpec((tb, HEAD_DIM), lambda s, n: (f(n), 0))

    qspec = pl.BlockSpec((tb, qw), lambda s, n: (s * nb + n, q_col // qw))
    kb, vb = k_col // kw, v_col // kw
    return pl.pallas_call(
        _attn_kernel,
        out_shape=jax.ShapeDtypeStruct((t, qw), BF16),
        grid=(n_seq, nb),
        in_specs=[qspec, kv(kb, prev), kv(kb, same), kv(kb, nxt), kv(vb, prev), kv(vb, same), kv(vb, nxt),
                  tab(prev), tab(prev), tab(same), tab(same), tab(nxt), tab(nxt),
                  pl.BlockSpec(memory_space=pltpu.SMEM),
                  pl.BlockSpec((1, qw), lambda s, n: (0, 0))],
        out_specs=pl.BlockSpec((tb, qw), lambda s, n: (s * nb + n, 0)),
        compiler_params=_params(("parallel", "parallel")),
        name="window_attention",
    )(p, p, p, p, p, p, p, cos, sin_signed, cos, sin_signed, cos, sin_signed, sink, norm_w.reshape(1, qw))


def rope_tables(seq_len):
    inv_freq = ROPE_THETA ** (-jnp.arange(0, HEAD_DIM, 2, dtype=F32) / HEAD_DIM)
    ang = jnp.arange(seq_len, dtype=F32)[:, None] * inv_freq[None, :]
    ang = jnp.concatenate([ang, ang], axis=-1)
    sign = jnp.concatenate([-jnp.ones((HEAD_DIM // 2,), F32), jnp.ones((HEAD_DIM // 2,), F32)])
    return jnp.cos(ang), jnp.sin(ang) * sign[None, :]


assert PEER_TOPK == 2 * SUBLANES
_PAIR_LIMIT = [PEER_TOPK // (a + 1) for a in range(PEER_TOPK)]
_N_MID = PEER_TOPK // 2 - 1
_CAND_ROWS = PEER_TOPK + SUBLANES * _N_MID + SUBLANES
assert all(lim <= SUBLANES for lim in _PAIR_LIMIT[1:]) and all(lim == 1 for lim in _PAIR_LIMIT[_N_MID + 1:])


def _extract_top(s, iota, k_out):
    n = s.shape[0]
    vals, idxs = [], []
    for _ in range(k_out):
        m = jnp.max(s, axis=0, keepdims=True)
        idx = jnp.min(jnp.where(s == m, iota, float(n)), axis=0, keepdims=True)
        s = jnp.where(iota == idx, -jnp.inf, s)
        vals.append(m)
        idxs.append(idx)
    return vals, idxs


def _query_route_kernel(x_ref, r_ref, wq_ref, sk_ref, i_ref, j_ref, g_ref, qy_a, qy_b, ti, tj, tg):
    tm = x_ref.shape[0]
    step = pl.program_id(0)
    slabs = wq_ref.shape[2] // HEAD_DIM
    heads_per_trip = slabs // 2
    kio = lax.broadcasted_iota(jnp.int32, (N_KEYS, LANES), 0).astype(F32)
    pio = lax.broadcasted_iota(jnp.int32, (_CAND_ROWS, LANES), 0).astype(F32)
    sub = lax.broadcasted_iota(jnp.int32, (SUBLANES, LANES), 0)

    @pl.when(step == 0)
    def _():
        qy_b[...] = jnp.zeros_like(qy_b)

    halves = tm // LANES
    assert heads_per_trip * halves == slabs, "one projection slab per (head, token half) unit"

    def unit_scores(h, half, q_read):
        tok = slice(half * LANES, (half + 1) * LANES)
        return [_dot_nt(sk_ref[h, c].astype(BF16), q_read[2 * h + c, tok, :].astype(BF16)) for c in range(2)]

    def unit_route(h, half, scores):
        row = pl.multiple_of(h * PEER_TOPK, PEER_TOPK)
        tok = slice(half * LANES, (half + 1) * LANES)
        (v0, i0), (v1, i1) = [_extract_top(s, kio, PEER_TOPK) for s in scores]
        v1_lo = jnp.concatenate(v1[:SUBLANES], axis=0)
        groups = [v0[0] + jnp.concatenate(v1, axis=0)]
        for a in range(1, _N_MID + 1):
            groups.append(jnp.where(sub < _PAIR_LIMIT[a], v0[a] + v1_lo, -jnp.inf))
        groups.append(jnp.concatenate(v0[_N_MID + 1:], axis=0) + v1[0])
        top_v, pos = _extract_top(jnp.concatenate(groups, axis=0), pio, PEER_TOPK)
        top_v, pos = jnp.concatenate(top_v, axis=0), jnp.concatenate(pos, axis=0)
        mid = jnp.floor((pos - PEER_TOPK) * (1.0 / SUBLANES))
        last = float(PEER_TOPK + SUBLANES * _N_MID)
        a_sel = jnp.where(pos < PEER_TOPK, 0.0, jnp.where(pos < last, 1.0 + mid, pos - (last - _N_MID - 1)))
        b_sel = jnp.where(pos < PEER_TOPK, pos,
                          jnp.where(pos < last, pos - PEER_TOPK - SUBLANES * mid, 0.0))
        ki = jnp.zeros_like(pos)
        kj = jnp.zeros_like(pos)
        for a in range(PEER_TOPK):
            ki = ki + jnp.where(a_sel == float(a), i0[a], 0.0)
            kj = kj + jnp.where(b_sel == float(a), i1[a], 0.0)
        e = jnp.exp(top_v - top_v[0:1, :])
        ti[pl.ds(row, PEER_TOPK), tok] = ki
        tj[pl.ds(row, PEER_TOPK), tok] = kj
        tg[pl.ds(row, PEER_TOPK), tok] = e / jnp.sum(e, axis=0, keepdims=True)

    def run(q_write, q_read):
        def trip(g, carry):
            units = [(g * heads_per_trip + u, half) for u in range(heads_per_trip) for half in range(halves)]
            x = x_ref[...]
            r = r_ref[...]
            scores = []
            for s, (h, half) in enumerate(units):
                scores.append(unit_scores(h, half, q_read))
                if s % 2 == 1:
                    acc = jnp.dot(x, wq_ref[g, :, (s - 1) * HEAD_DIM:(s + 1) * HEAD_DIM],
                                  preferred_element_type=F32)
                    q_write[g * slabs + s - 1] = acc[:, :HEAD_DIM] * r
                    q_write[g * slabs + s] = acc[:, HEAD_DIM:] * r
            for (h, half), sc in zip(units, scores):
                unit_route(h, half, sc)
            return carry

        lax.fori_loop(0, wq_ref.shape[0], trip, 0)

    @pl.when(lax.rem(step, 2) == 0)
    def _():
        run(qy_a, qy_b)

    @pl.when(lax.rem(step, 2) == 1)
    def _():
        run(qy_b, qy_a)

    i_ref[...] = ti[...].T
    j_ref[...] = tj[...].T
    g_ref[...] = tg[...].T


def peer_query_route(x, r, wq, sub_keys):
    t, d = x.shape
    tm = _row_tile(t, ROUTE_TILE)
    nt = t // tm
    slots = PEER_HEADS * PEER_TOPK
    out = jax.ShapeDtypeStruct((t, slots), F32)
    ospec = pl.BlockSpec((tm, slots), lambda s: (jnp.maximum(s - 1, 0), 0))
    full = pltpu.VMEM((slots, tm), F32)
    queries = pltpu.VMEM((2 * PEER_HEADS, tm, HEAD_DIM), F32)
    return pl.pallas_call(
        _query_route_kernel,
        out_shape=(out, out, out),
        grid=(nt + 1,),
        in_specs=[pl.BlockSpec((tm, d), lambda s: (jnp.minimum(s, nt - 1), 0)),
                  pl.BlockSpec((tm, LANES), lambda s: (jnp.minimum(s, nt - 1), 0)),
                  pl.BlockSpec(wq.shape, lambda s: (0, 0, 0), pipeline_mode=pl.Buffered(1)),
                  pl.BlockSpec(sub_keys.shape, lambda s: (0, 0, 0, 0))],
        out_specs=(ospec, ospec, ospec),
        scratch_shapes=[queries, queries, full, full, full],
        compiler_params=_params(("arbitrary",)),
        name="peer_query_route",
    )(x, r, wq, sub_keys)


def _coef_kernel(i_ref, j_ref, g_ref, c_ref, stage):
    tm = i_ref.shape[0]
    sub = lax.broadcasted_iota(jnp.int32, (N_KEYS, i_ref.shape[1]), 0).astype(F32)

    def group(t0, buf):
        ib = i_ref[pl.ds(t0, COEF_GROUP), :]
        jb = j_ref[pl.ds(t0, COEF_GROUP), :]
        gb = g_ref[pl.ds(t0, COEF_GROUP), :]
        for t in range(COEF_GROUP):
            ptg = jnp.where(sub == ib[t:t + 1, :], gb[t:t + 1, :], 0.0).astype(BF16)
            qt = jnp.where(sub == jb[t:t + 1, :], 1.0, 0.0).astype(BF16)
            buf[t * COEF_PITCH:t * COEF_PITCH + N_KEYS, :] = _dot_nt(ptg, qt)
        for i in range(N_KEYS):
            c_ref[i, pl.ds(t0, COEF_GROUP), :] = buf[pl.ds(i, COEF_GROUP, stride=COEF_PITCH), :].astype(c_ref.dtype)

    def trip(gi, carry):
        for u in range(COEF_UNROLL):
            group(pl.multiple_of((gi * COEF_UNROLL + u) * COEF_GROUP, COEF_GROUP), stage.at[u])
        return carry

    lax.fori_loop(0, tm // (COEF_GROUP * COEF_UNROLL), trip, 0)


def peer_coefficients(ki, kj, gate):
    t, slots = ki.shape
    tm = _row_tile(t, COEF_TILE)
    spec = pl.BlockSpec((tm, slots), lambda i: (i, 0))
    return pl.pallas_call(
        _coef_kernel,
        out_shape=jax.ShapeDtypeStruct((N_KEYS, t, N_KEYS), BF16),
        grid=(t // tm,),
        in_specs=[spec, spec, spec],
        out_specs=pl.BlockSpec((N_KEYS, tm, N_KEYS), lambda i: (0, i, 0)),
        scratch_shapes=[pltpu.VMEM((COEF_UNROLL, COEF_GROUP * COEF_PITCH, N_KEYS), F32)],
        compiler_params=_params(("parallel",)),
        name="peer_coefficients",
    )(ki, kj, gate)


def _gelu(x):
    return 0.5 * x * (1.0 + lax.erf(x * 0.7071067811865476))


def _peer_kernel(n_side, x_ref, r_ref, u_ref, v_ref, c_ref, *refs):
    o_ref = refs[n_side]

    @pl.when(pl.program_id(1) == 0)
    def _():
        o_ref[...] = jnp.zeros_like(o_ref)

    hid = _dot_nt(x_ref[...], u_ref[...])
    r = r_ref[...]
    w = jnp.concatenate([(c_ref[s].astype(F32) * _gelu(hid[:, s * N_KEYS:(s + 1) * N_KEYS] * r)).astype(BF16)
                         for s in range(c_ref.shape[0])], axis=1)
    o_ref[...] += jnp.dot(w, v_ref[...], preferred_element_type=F32)
    for w_ref, cast_ref in zip(refs[:n_side], refs[n_side + 1:]):
        cast_ref[...] = w_ref[...].astype(cast_ref.dtype)


def _side_cast_specs(w, layer, gi, gj):
    _, r, c = w.shape
    rows = BF16_ROWS
    while rows * gi * gj < r:
        rows *= 2
    assert r % rows == 0
    last = r // rows - 1
    src = pl.BlockSpec((None, rows, c), lambda i, j: (layer, jnp.minimum(i * gj + j, last), 0))
    dst = pl.BlockSpec((rows, c), lambda i, j: (jnp.minimum(i * gj + j, last), 0))
    return src, jax.ShapeDtypeStruct((r, c), BF16), dst


def peer_dense(x, r, u, v, coef, side_casts=(), layer=0):
    t, d = x.shape
    e = u.shape[0]
    tm, te = _row_tile(t, PEER_ROWS), PEER_EXPERTS
    gi, gj = t // tm, e // te
    in_specs = [pl.BlockSpec((tm, d), lambda i, j: (i, 0), pipeline_mode=pl.Buffered(1)),
                pl.BlockSpec((tm, LANES), lambda i, j: (i, 0)),
                pl.BlockSpec((te, d), lambda i, j: (j, 0)),
                pl.BlockSpec((te, d), lambda i, j: (j, 0)),
                pl.BlockSpec((te // N_KEYS, tm, N_KEYS), lambda i, j: (j, i, 0))]
    out_shape = [jax.ShapeDtypeStruct((t, d), F32)]
    out_specs = [pl.BlockSpec((tm, d), lambda i, j: (i, 0), pipeline_mode=pl.Buffered(1))]
    for w in side_casts:
        src, shape, dst = _side_cast_specs(w, layer, gi, gj)
        in_specs.append(src)
        out_shape.append(shape)
        out_specs.append(dst)
    res = pl.pallas_call(
        functools.partial(_peer_kernel, len(side_casts)),
        out_shape=tuple(out_shape),
        grid=(gi, gj),
        in_specs=in_specs,
        out_specs=tuple(out_specs),
        compiler_params=_params(("arbitrary", "arbitrary")),
        name="peer_dense",
    )(x, r, u, v, coef, *side_casts)
    return res if side_casts else res[0]


def kernel(x_prompt, x_sample, w_in, w_out, norm_mix, norm_ffn, lb_logits, hgrn_norm, attn_norm, attn_sink,
           peer_query, peer_sub_keys, peer_u, peer_v, final_norm):
    depth = w_in.shape[0]
    d_model = x_prompt.shape[-1]
    seq_len = x_prompt.shape[1]
    assert x_sample.shape[1] == seq_len, "prompt and sample sequences are stacked and must share a length"
    n_seq = x_prompt.shape[0] + x_sample.shape[0]
    xp = x_prompt.reshape(-1, d_model)
    xs = x_sample.reshape(-1, d_model)
    t_prompt, t_sample = xp.shape[0], xs.shape[0]
    hw = N_HGRN_HEADS * HEAD_DIM
    q_col = 5 * hw
    k_col = q_col + N_ATTN_HEADS * HEAD_DIM
    v_col = k_col + N_KV_HEADS * HEAD_DIM

    cos, sin_signed = rope_tables(seq_len)
    lbl_t = jnp.transpose(lb_logits.astype(F32), (1, 0, 2))

    h = rmsnorm_rows2(xp, xs, norm_mix[0], BF16)
    w_in_bf16 = cast_bf16(w_in, 0)
    x = None
    for l in range(depth):
        p, u_bf16, v_bf16, w_out_bf16 = matmul(h, w_in_bf16, side_casts=(peer_u, peer_v, w_out), layer=l)
        o_h = hgrn2(p, lbl_t, hgrn_norm[l], l, n_seq, seq_len)
        o_a = window_attention(p, cos, sin_signed, attn_sink[l].astype(F32), attn_norm[l], n_seq, seq_len,
                               q_col, k_col, v_col)
        if l == 0:
            x, xg, r = matmul2_residual_norm(o_h, o_a, w_out_bf16, norm_ffn[l], xp, xs)
        else:
            x, xg, r = matmul2_residual_norm(o_h, o_a, w_out_bf16, norm_ffn[l], x)
        ki, kj, gate = peer_query_route(xg, r, cast_bf16(peer_query, l, col_panels=True), peer_sub_keys[l])
        coef = peer_coefficients(ki, kj, gate)
        if l + 1 < depth:
            po, w_in_bf16 = peer_dense(xg, r, u_bf16, v_bf16, coef, side_casts=(w_in,), layer=l + 1)
            x, h = add_rmsnorm_rows(x, po, norm_mix[l + 1], BF16)
        else:
            po = peer_dense(xg, r, u_bf16, v_bf16, coef)
    y_prompt = add_rmsnorm_final(x, po, final_norm, 0, t_prompt)
    y_sample = add_rmsnorm_final(x, po, final_norm, t_prompt, t_sample)
    return (y_prompt.reshape(x_prompt.shape), y_sample.reshape(x_sample.shape))
```

```python
import functools

import jax
import jax.numpy as jnp
from jax import lax
from jax.experimental import pallas as pl
from jax.experimental.pallas import tpu as pltpu

F32 = jnp.float32
BF16 = jnp.bfloat16

LANES = 128
SUBLANES = 8
BF16_ROWS = 2 * SUBLANES
HEAD_DIM = LANES
N_HGRN_HEADS = 16
N_ATTN_HEADS = 16
N_KV_HEADS = 4
GQA_REP = N_ATTN_HEADS // N_KV_HEADS
WINDOW = 128
ROPE_THETA = 10000.0
HGRN_SUB = 16
N_KEYS = 128
PEER_HEADS = 8
PEER_TOPK = 16
EPS = 1e-6
NEG_INF = -1e30
LOG2_E = 1.4426950408889634

VMEM_LIMIT = 56 * 1024 * 1024

NORM_ROWS = 256
MM_ROWS = 1024
MM_COLS = 512
CAST_BLOCK = 1024
HGRN_BLOCK = 128
HGRN_GROUP = 16
ATTN_BLOCK = WINDOW
ROUTE_TILE = 256
COEF_TILE = 256
COEF_GROUP = 16
COEF_PITCH = N_KEYS + SUBLANES
COEF_UNROLL = 8
PEER_ROWS = 1024
PEER_EXPERTS = 512


def _params(sem):
    return pltpu.CompilerParams(dimension_semantics=sem, vmem_limit_bytes=VMEM_LIMIT)


def _sigmoid(x):
    return 1.0 / (1.0 + jnp.exp(-x))


def _dot_nt(a, b):
    return lax.dot_general(a, b, (((1,), (1,)), ((), ())), preferred_element_type=F32)


def _row_tile(t, want):
    return want if t % want == 0 else t


def _common_tile(want, *sizes):
    while any(n % want for n in sizes):
        want //= 2
    return want


def _rms(x, g):
    return x * lax.rsqrt(jnp.mean(x * x, axis=-1, keepdims=True) + EPS) * g


def _cast_kernel(w_ref, o_ref):
    o_ref[...] = w_ref[...].astype(o_ref.dtype)


def cast_bf16(w, layer, col_panels=False):
    _, r, c = w.shape
    br, bc = _row_tile(r, CAST_BLOCK), _row_tile(c, CAST_BLOCK)
    if col_panels:
        out_shape = jax.ShapeDtypeStruct((c // bc, r, bc), BF16)
        out_spec = pl.BlockSpec((None, br, bc), lambda i, j: (j, i, 0))
    else:
        out_shape = jax.ShapeDtypeStruct((r, c), BF16)
        out_spec = pl.BlockSpec((br, bc), lambda i, j: (i, j))
    return pl.pallas_call(
        _cast_kernel,
        out_shape=out_shape,
        grid=(r // br, c // bc),
        in_specs=[pl.BlockSpec((None, br, bc), lambda i, j: (layer, i, j))],
        out_specs=out_spec,
        compiler_params=_params(("parallel", "parallel")),
        name="cast_bf16",
    )(w)


def _norm2_kernel(n_a, xa_ref, xb_ref, g_ref, h_ref):
    x = jnp.where(pl.program_id(0) < n_a, xa_ref[...], xb_ref[...])
    h_ref[...] = _rms(x, g_ref[...]).astype(h_ref.dtype)


def _norm_kernel(x_ref, g_ref, h_ref):
    h_ref[...] = _rms(x_ref[...], g_ref[...]).astype(h_ref.dtype)


def _addnorm_kernel(x_ref, y_ref, g_ref, s_ref, h_ref):
    x = x_ref[...] + y_ref[...]
    s_ref[...] = x
    h_ref[...] = _rms(x, g_ref[...]).astype(h_ref.dtype)


def _addnorm_final_kernel(x_ref, y_ref, g_ref, h_ref):
    h_ref[...] = _rms(x_ref[...] + y_ref[...], g_ref[...]).astype(h_ref.dtype)


def _two_source_specs(ta, tm, width):
    n_a = ta // tm
    spec_a = pl.BlockSpec((tm, width), lambda i, *_: (jnp.minimum(i, n_a - 1), 0))
    spec_b = pl.BlockSpec((tm, width), lambda i, *_: (jnp.maximum(i - n_a, 0), 0))
    return n_a, spec_a, spec_b


def rmsnorm_rows2(xa, xb, g, out_dtype):
    ta, d = xa.shape
    t = ta + xb.shape[0]
    tm = _common_tile(NORM_ROWS, ta, t)
    n_a, spec_a, spec_b = _two_source_specs(ta, tm, d)
    return pl.pallas_call(
        functools.partial(_norm2_kernel, n_a),
        out_shape=jax.ShapeDtypeStruct((t, d), out_dtype),
        grid=(t // tm,),
        in_specs=[spec_a, spec_b, pl.BlockSpec((1, d), lambda i: (0, 0))],
        out_specs=pl.BlockSpec((tm, d), lambda i: (i, 0)),
        compiler_params=_params(("parallel",)),
        name="rmsnorm2",
    )(xa, xb, g.reshape(1, d))


def rmsnorm_rows(x, g, out_dtype):
    t, d = x.shape
    tm = _row_tile(t, NORM_ROWS)
    row = pl.BlockSpec((tm, d), lambda i: (i, 0))
    return pl.pallas_call(
        _norm_kernel,
        out_shape=jax.ShapeDtypeStruct((t, d), out_dtype),
        grid=(t // tm,),
        in_specs=[row, pl.BlockSpec((1, d), lambda i: (0, 0))],
        out_specs=row,
        compiler_params=_params(("parallel",)),
        name="rmsnorm",
    )(x, g.reshape(1, d))


def add_rmsnorm_rows(x, y, g, out_dtype):
    t, d = x.shape
    tm = _row_tile(t, NORM_ROWS)
    row = pl.BlockSpec((tm, d), lambda i: (i, 0))
    return pl.pallas_call(
        _addnorm_kernel,
        out_shape=(jax.ShapeDtypeStruct((t, d), F32), jax.ShapeDtypeStruct((t, d), out_dtype)),
        grid=(t // tm,),
        in_specs=[row, row, pl.BlockSpec((1, d), lambda i: (0, 0))],
        out_specs=(row, row),
        compiler_params=_params(("parallel",)),
        name="add_rmsnorm",
    )(x, y, g.reshape(1, d))


def add_rmsnorm_final(x, y, g, row_start, n_rows):
    d = x.shape[1]
    tm = _row_tile(n_rows, NORM_ROWS)
    off = row_start // tm
    src = pl.BlockSpec((tm, d), lambda i: (i + off, 0))
    return pl.pallas_call(
        _addnorm_final_kernel,
        out_shape=jax.ShapeDtypeStruct((n_rows, d), F32),
        grid=(n_rows // tm,),
        in_specs=[src, src, pl.BlockSpec((1, d), lambda i: (0, 0))],
        out_specs=pl.BlockSpec((tm, d), lambda i: (i, 0)),
        compiler_params=_params(("parallel",)),
        name="add_rmsnorm_final",
    )(x, y, g.reshape(1, d))


def _mm_cast_kernel(n_side, a_ref, b_ref, *refs):
    o_ref = refs[n_side]
    o_ref[...] = jnp.dot(a_ref[...], b_ref[...], preferred_element_type=F32)
    for w_ref, c_ref in zip(refs[:n_side], refs[n_side + 1:]):
        c_ref[...] = w_ref[...].astype(c_ref.dtype)


def matmul(a, b, side_casts=(), layer=0):
    t, k = a.shape
    n = b.shape[1]
    tm, tn = _row_tile(t, MM_ROWS), MM_COLS
    gi, gj = t // tm, n // tn
    in_specs = [pl.BlockSpec((tm, k), lambda i, j: (i, 0)), pl.BlockSpec((k, tn), lambda i, j: (0, j))]
    out_shape = [jax.ShapeDtypeStruct((t, n), F32)]
    out_specs = [pl.BlockSpec((tm, tn), lambda i, j: (i, j))]
    for w in side_casts:
        src, shape, dst = _side_cast_specs(w, layer, gi, gj)
        in_specs.append(src)
        out_shape.append(shape)
        out_specs.append(dst)
    res = pl.pallas_call(
        functools.partial(_mm_cast_kernel, len(side_casts)),
        out_shape=tuple(out_shape),
        grid=(gi, gj),
        in_specs=in_specs,
        out_specs=tuple(out_specs),
        compiler_params=_params(("arbitrary", "arbitrary")),
        name="matmul",
    )(a, b, *side_casts)
    return res if side_casts else res[0]


def _mm2_res_kernel(n_a, n_cols, a1_ref, a2_ref, b1_ref, b2_ref, g_ref, *refs):
    res_refs, (x_ref, xg_ref, r_ref, ss) = refs[:-4], refs[-4:]
    j = pl.program_id(1)
    acc = jnp.dot(a1_ref[...], b1_ref[...], preferred_element_type=F32)
    acc = acc + jnp.dot(a2_ref[...], b2_ref[...], preferred_element_type=F32)
    if len(res_refs) == 1:
        x = acc + res_refs[0][...]
    else:
        x = acc + jnp.where(pl.program_id(0) < n_a, res_refs[0][...], res_refs[1][...])
    x_ref[...] = x
    xg_ref[...] = (x * g_ref[...]).astype(xg_ref.dtype)
    sq = x * x
    part = sq[:, :LANES]
    for s in range(1, sq.shape[1] // LANES):
        part = part + sq[:, s * LANES:(s + 1) * LANES]

    @pl.when(j == 0)
    def _():
        ss[...] = part

    @pl.when(j > 0)
    def _():
        ss[...] += part

    @pl.when(j == pl.num_programs(1) - 1)
    def _():
        total = jnp.sum(ss[...], axis=-1, keepdims=True)
        r_ref[...] = jnp.broadcast_to(lax.rsqrt(total * (1.0 / n_cols) + EPS), r_ref.shape)


def matmul2_residual_norm(a1, a2, b, gain, res, res_b=None):
    t, kh = a1.shape
    n = b.shape[1]
    tm = _row_tile(t, MM_ROWS) if res_b is None else _common_tile(MM_ROWS, res.shape[0], t)
    tn = MM_COLS
    specs = [
        pl.BlockSpec((tm, kh), lambda i, j: (i, 0)),
        pl.BlockSpec((tm, kh), lambda i, j: (i, 0)),
        pl.BlockSpec((kh, tn), lambda i, j: (0, j)),
        pl.BlockSpec((kh, tn), lambda i, j: (1, j)),
        pl.BlockSpec((1, tn), lambda i, j: (0, j)),
    ]
    tile = pl.BlockSpec((tm, tn), lambda i, j: (i, j))
    if res_b is None:
        n_a = 0
        specs.append(tile)
        args = (a1, a2, b, b, gain.reshape(1, n), res)
    else:
        n_a = res.shape[0] // tm
        specs.append(pl.BlockSpec((tm, tn), lambda i, j: (jnp.minimum(i, n_a - 1), j)))
        specs.append(pl.BlockSpec((tm, tn), lambda i, j: (jnp.maximum(i - n_a, 0), j)))
        args = (a1, a2, b, b, gain.reshape(1, n), res, res_b)
    return pl.pallas_call(
        functools.partial(_mm2_res_kernel, n_a, n),
        out_shape=(jax.ShapeDtypeStruct((t, n), F32), jax.ShapeDtypeStruct((t, n), BF16),
                   jax.ShapeDtypeStruct((t, LANES), F32)),
        grid=(t // tm, n // tn),
        in_specs=specs,
        out_specs=(tile, tile, pl.BlockSpec((tm, LANES), lambda i, j: (i, 0))),
        scratch_shapes=[pltpu.VMEM((tm, LANES), F32)],
        compiler_params=_params(("parallel", "arbitrary")),
        name="matmul2_residual_norm",
    )(*args)


def _hgrn_masks(cb, width, reverse):
    ti = lax.broadcasted_iota(jnp.int32, (cb, width), 0)
    ji = lax.broadcasted_iota(jnp.int32, (cb, width), 1) % HEAD_DIM
    same = (ti // HGRN_SUB) == (ji // HGRN_SUB)
    masks = [jnp.logical_and(same, (ji >= ti) if reverse else (ji <= ti))]
    hs = HGRN_SUB
    while 2 * hs <= cb:
        grp = (ti // (2 * hs)) == (ji // (2 * hs))
        t_hi = (ti % (2 * hs)) >= hs
        j_hi = (ji % (2 * hs)) >= hs
        if reverse:
            sel = jnp.logical_and(jnp.logical_not(t_hi), j_hi)
        else:
            sel = jnp.logical_and(t_hi, jnp.logical_not(j_hi))
        masks.append(jnp.logical_and(grp, sel))
        hs *= 2
    t1 = lax.broadcasted_iota(jnp.int32, (cb, cb), 0)
    j1 = lax.broadcasted_iota(jnp.int32, (cb, cb), 1)
    tri = ((j1 >= t1) if reverse else (j1 <= t1)).astype(BF16)
    return masks, tri


def _ref_rows(b, group, idx):
    cb, w = b.shape
    g = b.reshape(cb // group, group, w)
    r = jnp.broadcast_to(g[:, idx:idx + 1, :], g.shape)
    return r.reshape(cb, w)


def _cumsum_rows(tri, a):
    w = a.shape[1]
    a1 = a.astype(BF16)
    r1 = a - a1.astype(F32)
    a2 = r1.astype(BF16)
    a3 = (r1 - a2.astype(F32)).astype(BF16)
    out = jnp.dot(tri, jnp.concatenate([a1, a2, a3], axis=1), preferred_element_type=F32)
    return out[:, :w] + out[:, w:2 * w] + out[:, 2 * w:]


def _heads(x):
    return [x[:, h * HEAD_DIM:(h + 1) * HEAD_DIM] for h in range(x.shape[1] // HEAD_DIM)]


def _hgrn_group(qz, fz, v, lb, st_ref, first_head, masks, tri, reverse):
    cb = qz.shape[0]
    q = qz * _sigmoid(qz)
    f = lb + (1.0 - lb) * _sigmoid(fz)
    a = jnp.log(f) * LOG2_E
    k = 1.0 - f
    b = _cumsum_rows(tri, a)
    b_tot = b[0:1, :] if reverse else b[cb - 1:cb, :]

    def scores(eq, ek):
        qs, ks = _heads((q * eq).astype(BF16)), _heads((k * ek).astype(BF16))
        return jnp.concatenate([_dot_nt(qh, kh) for qh, kh in zip(qs, ks)], axis=1)

    r0 = _ref_rows(b, HGRN_SUB, HGRN_SUB // 2 if reverse else HGRN_SUB // 2 - 1)
    att = jnp.where(masks[0], scores(jnp.exp2(b - r0), jnp.exp2(r0 - b)), 0.0)
    hs = HGRN_SUB
    for mask in masks[1:]:
        e = jnp.exp2(-jnp.abs(b - _ref_rows(b, 2 * hs, hs if reverse else hs - 1)))
        att = jnp.where(mask, scores(e, e), att)
        hs *= 2
    att_h = _heads(att.astype(BF16))
    v_h = _heads(v)
    q_dec = _heads((q * jnp.exp2(b)).astype(BF16))
    k_dec = _heads((k * jnp.exp2(b_tot - b)).astype(BF16))
    decay = _heads(jnp.exp2(b_tot))
    outs = []
    for h in range(len(v_h)):
        st = st_ref[first_head + h]
        o = jnp.dot(att_h[h], v_h[h].astype(BF16), preferred_element_type=F32)
        outs.append(o + _dot_nt(q_dec[h], st.astype(BF16)))
        st_ref[first_head + h] = st * decay[h] + jnp.dot(v_h[h].T.astype(BF16), k_dec[h],
                                                         preferred_element_type=F32)
    return jnp.concatenate(outs, axis=1)


def _lower_bound(lbl, layer):
    if layer == 0:
        return jnp.zeros((1, lbl.shape[1]), F32)
    m = jnp.max(lbl, axis=0, keepdims=True)
    e = jnp.exp(lbl - m)
    sm = e / jnp.sum(e, axis=0, keepdims=True)
    return jnp.sum(sm[1:layer + 1, :], axis=0, keepdims=True)


def _hgrn_fwd_kernel(layer, q_ref, f_ref, i_ref, lbl_ref, o_ref, st_ref):
    @pl.when(pl.program_id(1) == 0)
    def _():
        st_ref[...] = jnp.zeros_like(st_ref)

    gw = HGRN_GROUP * HEAD_DIM
    masks, tri = _hgrn_masks(q_ref.shape[0], gw, False)
    lb = _lower_bound(lbl_ref[...], layer)
    for g in range(N_HGRN_HEADS // HGRN_GROUP):
        sl = slice(g * gw, (g + 1) * gw)
        o_ref[:, sl] = _hgrn_group(q_ref[:, sl], f_ref[:, sl], i_ref[:, sl], lb[:, sl], st_ref,
                                   g * HGRN_GROUP, masks, tri, False)


def _hgrn_bwd_kernel(layer, q_ref, f_ref, i_ref, g_ref, of_ref, lbl_ref, nw_ref, o_ref, st_ref):
    @pl.when(pl.program_id(1) == 0)
    def _():
        st_ref[...] = jnp.zeros_like(st_ref)

    gw = HGRN_GROUP * HEAD_DIM
    masks, tri = _hgrn_masks(q_ref.shape[0], gw, True)
    lb = _lower_bound(lbl_ref[...], layer)
    for g in range(N_HGRN_HEADS // HGRN_GROUP):
        sl = slice(g * gw, (g + 1) * gw)
        o = _hgrn_group(q_ref[:, sl], f_ref[:, sl], i_ref[:, sl], lb[:, sl], st_ref,
                        g * HGRN_GROUP, masks, tri, True) + of_ref[:, sl]
        o = jnp.concatenate([oh * lax.rsqrt(jnp.mean(oh * oh, axis=-1, keepdims=True) + EPS)
                             for oh in _heads(o)], axis=1) * nw_ref[:, sl]
        gz = g_ref[:, sl]
        o_ref[:, sl] = (o * (gz * _sigmoid(gz))).astype(o_ref.dtype)


def hgrn2(p, lb_logits_t, norm_w, layer, n_seq, seq_len):
    t = p.shape[0]
    w = N_HGRN_HEADS * HEAD_DIM
    cb = HGRN_BLOCK
    nb = seq_len // cb
    depth = lb_logits_t.shape[1]

    def col(j, reverse):
        if reverse:
            return pl.BlockSpec((cb, w), lambda s, c: (s * nb + nb - 1 - c, j))
        return pl.BlockSpec((cb, w), lambda s, c: (s * nb + c, j))

    def lbl(d):
        return pl.BlockSpec((None, depth, w), lambda s, c: (d, 0, 0))

    scratch = [pltpu.VMEM((N_HGRN_HEADS, HEAD_DIM, HEAD_DIM), F32)]
    o_f = pl.pallas_call(
        functools.partial(_hgrn_fwd_kernel, layer),
        out_shape=jax.ShapeDtypeStruct((t, w), F32),
        grid=(n_seq, nb),
        in_specs=[col(0, False), col(1, False), col(3, False), lbl(0)],
        out_specs=col(0, False),
        scratch_shapes=scratch,
        compiler_params=_params(("parallel", "arbitrary")),
        name="hgrn_fwd",
    )(p, p, p, lb_logits_t)
    return pl.pallas_call(
        functools.partial(_hgrn_bwd_kernel, layer),
        out_shape=jax.ShapeDtypeStruct((t, w), BF16),
        grid=(n_seq, nb),
        in_specs=[col(0, True), col(2, True), col(3, True), col(4, True), col(0, True), lbl(1),
                  pl.BlockSpec((1, w), lambda s, c: (0, 0))],
        out_specs=col(0, True),
        scratch_shapes=scratch,
        compiler_params=_params(("parallel", "arbitrary")),
        name="hgrn_bwd",
    )(p, p, p, p, o_f, lb_logits_t, norm_w.reshape(1, w))


def _rope(x, cos, sin_signed):
    return x * cos + pltpu.roll(x, HEAD_DIM // 2, axis=1) * sin_signed


def _attn_kernel(q_ref, kp_ref, kc_ref, kn_ref, vp_ref, vc_ref, vn_ref,
                 cp_ref, sp_ref, cc_ref, sc_ref, cn_ref, sn_ref, sink_ref, nw_ref, o_ref):
    n = pl.program_id(1)
    nb = pl.num_programs(1)
    tb = q_ref.shape[0]
    rows = GQA_REP * tb
    ri = lax.broadcasted_iota(jnp.int32, (rows, tb), 0) % tb
    ci = lax.broadcasted_iota(jnp.int32, (rows, tb), 1)
    head_of_row = lax.broadcasted_iota(jnp.int32, (rows, 1), 0) // tb
    m_prev = jnp.logical_and(ci >= ri, n > 0)
    m_next = jnp.logical_and(ci <= ri, n < nb - 1)
    scale = HEAD_DIM ** -0.5
    cc, sc = cc_ref[...], sc_ref[...]
    for g in range(N_KV_HEADS):
        sl = slice(g * HEAD_DIM, (g + 1) * HEAD_DIM)
        kcat = jnp.concatenate([
            _rope(kp_ref[:, sl], cp_ref[...], sp_ref[...]),
            _rope(kc_ref[:, sl], cc, sc),
            _rope(kn_ref[:, sl], cn_ref[...], sn_ref[...]),
        ], axis=0).astype(BF16)
        vcat = jnp.concatenate([vp_ref[:, sl], vc_ref[:, sl], vn_ref[:, sl]], axis=0).astype(BF16)
        heads = [g * GQA_REP + r for r in range(GQA_REP)]
        q = jnp.concatenate([_rope(q_ref[:, h * HEAD_DIM:(h + 1) * HEAD_DIM], cc, sc) for h in heads],
                            axis=0).astype(BF16)
        s = _dot_nt(q, kcat) * scale
        s = jnp.concatenate([jnp.where(m_prev, s[:, :tb], NEG_INF), s[:, tb:2 * tb],
                             jnp.where(m_next, s[:, 2 * tb:], NEG_INF)], axis=1)
        sink = jnp.zeros((rows, 1), F32)
        for r, h in enumerate(heads):
            sink = jnp.where(head_of_row == r, sink_ref[h], sink)
        m = jnp.maximum(jnp.max(s, axis=-1, keepdims=True), sink)
        e = jnp.exp(s - m)
        den = jnp.sum(e, axis=-1, keepdims=True) + jnp.exp(sink - m)
        o = jnp.dot(e.astype(BF16), vcat, preferred_element_type=F32) / den
        o = o * lax.rsqrt(jnp.mean(o * o, axis=-1, keepdims=True) + EPS)
        for r, h in enumerate(heads):
            hs = slice(h * HEAD_DIM, (h + 1) * HEAD_DIM)
            o_ref[:, hs] = (o[r * tb:(r + 1) * tb, :] * nw_ref[:, hs]).astype(o_ref.dtype)


def window_attention(p, cos, sin_signed, sink, norm_w, n_seq, seq_len, q_col, k_col, v_col):
    t = p.shape[0]
    tb = ATTN_BLOCK
    nb = seq_len // tb
    qw = N_ATTN_HEADS * HEAD_DIM
    kw = N_KV_HEADS * HEAD_DIM

    def prev(n):
        return jnp.maximum(n - 1, 0)

    def nxt(n):
        return jnp.minimum(n + 1, nb - 1)

    def same(n):
        return n

    def kv(colblk, f):
        return pl.BlockSpec((tb, kw), lambda s, n: (s * nb + f(n), colblk))

    def tab(f):
        return pl.BlockSpec((tb, HEAD_DIM), lambda s, n: (f(n), 0))

    qspec = pl.BlockSpec((tb, qw), lambda s, n: (s * nb + n, q_col // qw))
    kb, vb = k_col // kw, v_col // kw
    return pl.pallas_call(
        _attn_kernel,
        out_shape=jax.ShapeDtypeStruct((t, qw), BF16),
        grid=(n_seq, nb),
        in_specs=[qspec, kv(kb, prev), kv(kb, same), kv(kb, nxt), kv(vb, prev), kv(vb, same), kv(vb, nxt),
                  tab(prev), tab(prev), tab(same), tab(same), tab(nxt), tab(nxt),
                  pl.BlockSpec(memory_space=pltpu.SMEM),
                  pl.BlockSpec((1, qw), lambda s, n: (0, 0))],
        out_specs=pl.BlockSpec((tb, qw), lambda s, n: (s * nb + n, 0)),
        compiler_params=_params(("parallel", "parallel")),
        name="window_attention",
    )(p, p, p, p, p, p, p, cos, sin_signed, cos, sin_signed, cos, sin_signed, sink, norm_w.reshape(1, qw))


def rope_tables(seq_len):
    inv_freq = ROPE_THETA ** (-jnp.arange(0, HEAD_DIM, 2, dtype=F32) / HEAD_DIM)
    ang = jnp.arange(seq_len, dtype=F32)[:, None] * inv_freq[None, :]
    ang = jnp.concatenate([ang, ang], axis=-1)
    sign = jnp.concatenate([-jnp.ones((HEAD_DIM // 2,), F32), jnp.ones((HEAD_DIM // 2,), F32)])
    return jnp.cos(ang), jnp.sin(ang) * sign[None, :]


assert PEER_TOPK == 2 * SUBLANES
_PAIR_LIMIT = [PEER_TOPK // (a + 1) for a in range(PEER_TOPK)]
_N_MID = PEER_TOPK // 2 - 1
_CAND_ROWS = PEER_TOPK + SUBLANES * _N_MID + SUBLANES
assert all(lim <= SUBLANES for lim in _PAIR_LIMIT[1:]) and all(lim == 1 for lim in _PAIR_LIMIT[_N_MID + 1:])


def _extract_top(s, iota, k_out):
    n = s.shape[0]
    vals, idxs = [], []
    for _ in range(k_out):
        m = jnp.max(s, axis=0, keepdims=True)
        idx = jnp.min(jnp.where(s == m, iota, float(n)), axis=0, keepdims=True)
        s = jnp.where(iota == idx, -jnp.inf, s)
        vals.append(m)
        idxs.append(idx)
    return vals, idxs


def _query_route_kernel(x_ref, r_ref, wq_ref, sk_ref, i_ref, j_ref, g_ref, qy_a, qy_b, ti, tj, tg):
    tm = x_ref.shape[0]
    step = pl.program_id(0)
    slabs = wq_ref.shape[2] // HEAD_DIM
    heads_per_trip = slabs // 2
    kio = lax.broadcasted_iota(jnp.int32, (N_KEYS, LANES), 0).astype(F32)
    pio = lax.broadcasted_iota(jnp.int32, (_CAND_ROWS, LANES), 0).astype(F32)
    sub = lax.broadcasted_iota(jnp.int32, (SUBLANES, LANES), 0)

    @pl.when(step == 0)
    def _():
        qy_b[...] = jnp.zeros_like(qy_b)

    halves = tm // LANES
    assert heads_per_trip * halves == slabs, "one projection slab per (head, token half) unit"

    def unit_scores(h, half, q_read):
        tok = slice(half * LANES, (half + 1) * LANES)
        return [_dot_nt(sk_ref[h, c].astype(BF16), q_read[2 * h + c, tok, :].astype(BF16)) for c in range(2)]

    def unit_route(h, half, scores):
        row = pl.multiple_of(h * PEER_TOPK, PEER_TOPK)
        tok = slice(half * LANES, (half + 1) * LANES)
        (v0, i0), (v1, i1) = [_extract_top(s, kio, PEER_TOPK) for s in scores]
        v1_lo = jnp.concatenate(v1[:SUBLANES], axis=0)
        groups = [v0[0] + jnp.concatenate(v1, axis=0)]
        for a in range(1, _N_MID + 1):
            groups.append(jnp.where(sub < _PAIR_LIMIT[a], v0[a] + v1_lo, -jnp.inf))
        groups.append(jnp.concatenate(v0[_N_MID + 1:], axis=0) + v1[0])
        top_v, pos = _extract_top(jnp.concatenate(groups, axis=0), pio, PEER_TOPK)
        top_v, pos = jnp.concatenate(top_v, axis=0), jnp.concatenate(pos, axis=0)
        mid = jnp.floor((pos - PEER_TOPK) * (1.0 / SUBLANES))
        last = float(PEER_TOPK + SUBLANES * _N_MID)
        a_sel = jnp.where(pos < PEER_TOPK, 0.0, jnp.where(pos < last, 1.0 + mid, pos - (last - _N_MID - 1)))
        b_sel = jnp.where(pos < PEER_TOPK, pos,
                          jnp.where(pos < last, pos - PEER_TOPK - SUBLANES * mid, 0.0))
        ki = jnp.zeros_like(pos)
        kj = jnp.zeros_like(pos)
        for a in range(PEER_TOPK):
            ki = ki + jnp.where(a_sel == float(a), i0[a], 0.0)
            kj = kj + jnp.where(b_sel == float(a), i1[a], 0.0)
        e = jnp.exp(top_v - top_v[0:1, :])
        ti[pl.ds(row, PEER_TOPK), tok] = ki
        tj[pl.ds(row, PEER_TOPK), tok] = kj
        tg[pl.ds(row, PEER_TOPK), tok] = e / jnp.sum(e, axis=0, keepdims=True)

    def run(q_write, q_read):
        def trip(g, carry):
            units = [(g * heads_per_trip + u, half) for u in range(heads_per_trip) for half in range(halves)]
            x = x_ref[...]
            r = r_ref[...]
            scores = []
            for s, (h, half) in enumerate(units):
                scores.append(unit_scores(h, half, q_read))
                if s % 2 == 1:
                    acc = jnp.dot(x, wq_ref[g, :, (s - 1) * HEAD_DIM:(s + 1) * HEAD_DIM],
                                  preferred_element_type=F32)
                    q_write[g * slabs + s - 1] = acc[:, :HEAD_DIM] * r
                    q_write[g * slabs + s] = acc[:, HEAD_DIM:] * r
            for (h, half), sc in zip(units, scores):
                unit_route(h, half, sc)
            return carry

        lax.fori_loop(0, wq_ref.shape[0], trip, 0)

    @pl.when(lax.rem(step, 2) == 0)
    def _():
        run(qy_a, qy_b)

    @pl.when(lax.rem(step, 2) == 1)
    def _():
        run(qy_b, qy_a)

    i_ref[...] = ti[...].T
    j_ref[...] = tj[...].T
    g_ref[...] = tg[...].T


def peer_query_route(x, r, wq, sub_keys):
    t, d = x.shape
    tm = _row_tile(t, ROUTE_TILE)
    nt = t // tm
    slots = PEER_HEADS * PEER_TOPK
    out = jax.ShapeDtypeStruct((t, slots), F32)
    ospec = pl.BlockSpec((tm, slots), lambda s: (jnp.maximum(s - 1, 0), 0))
    full = pltpu.VMEM((slots, tm), F32)
    queries = pltpu.VMEM((2 * PEER_HEADS, tm, HEAD_DIM), F32)
    return pl.pallas_call(
        _query_route_kernel,
        out_shape=(out, out, out),
        grid=(nt + 1,),
        in_specs=[pl.BlockSpec((tm, d), lambda s: (jnp.minimum(s, nt - 1), 0)),
                  pl.BlockSpec((tm, LANES), lambda s: (jnp.minimum(s, nt - 1), 0)),
                  pl.BlockSpec(wq.shape, lambda s: (0, 0, 0), pipeline_mode=pl.Buffered(1)),
                  pl.BlockSpec(sub_keys.shape, lambda s: (0, 0, 0, 0))],
        out_specs=(ospec, ospec, ospec),
        scratch_shapes=[queries, queries, full, full, full],
        compiler_params=_params(("arbitrary",)),
        name="peer_query_route",
    )(x, r, wq, sub_keys)


def _coef_kernel(i_ref, j_ref, g_ref, c_ref, stage):
    tm = i_ref.shape[0]
    sub = lax.broadcasted_iota(jnp.int32, (N_KEYS, i_ref.shape[1]), 0).astype(F32)

    def group(t0, buf):
        ib = i_ref[pl.ds(t0, COEF_GROUP), :]
        jb = j_ref[pl.ds(t0, COEF_GROUP), :]
        gb = g_ref[pl.ds(t0, COEF_GROUP), :]
        for t in range(COEF_GROUP):
            ptg = jnp.where(sub == ib[t:t + 1, :], gb[t:t + 1, :], 0.0).astype(BF16)
            qt = jnp.where(sub == jb[t:t + 1, :], 1.0, 0.0).astype(BF16)
            buf[t * COEF_PITCH:t * COEF_PITCH + N_KEYS, :] = _dot_nt(ptg, qt)
        for i in range(N_KEYS):
            c_ref[i, pl.ds(t0, COEF_GROUP), :] = buf[pl.ds(i, COEF_GROUP, stride=COEF_PITCH), :].astype(c_ref.dtype)

    def trip(gi, carry):
        for u in range(COEF_UNROLL):
            group(pl.multiple_of((gi * COEF_UNROLL + u) * COEF_GROUP, COEF_GROUP), stage.at[u])
        return carry

    lax.fori_loop(0, tm // (COEF_GROUP * COEF_UNROLL), trip, 0)


def peer_coefficients(ki, kj, gate):
    t, slots = ki.shape
    tm = _row_tile(t, COEF_TILE)
    spec = pl.BlockSpec((tm, slots), lambda i: (i, 0))
    return pl.pallas_call(
        _coef_kernel,
        out_shape=jax.ShapeDtypeStruct((N_KEYS, t, N_KEYS), BF16),
        grid=(t // tm,),
        in_specs=[spec, spec, spec],
        out_specs=pl.BlockSpec((N_KEYS, tm, N_KEYS), lambda i: (0, i, 0)),
        scratch_shapes=[pltpu.VMEM((COEF_UNROLL, COEF_GROUP * COEF_PITCH, N_KEYS), F32)],
        compiler_params=_params(("parallel",)),
        name="peer_coefficients",
    )(ki, kj, gate)


def _gelu(x):
    return 0.5 * x * (1.0 + lax.erf(x * 0.7071067811865476))


def _peer_kernel(n_side, x_ref, r_ref, u_ref, v_ref, c_ref, *refs):
    o_ref = refs[n_side]

    @pl.when(pl.program_id(1) == 0)
    def _():
        o_ref[...] = jnp.zeros_like(o_ref)

    hid = _dot_nt(x_ref[...], u_ref[...])
    r = r_ref[...]
    w = jnp.concatenate([(c_ref[s].astype(F32) * _gelu(hid[:, s * N_KEYS:(s + 1) * N_KEYS] * r)).astype(BF16)
                         for s in range(c_ref.shape[0])], axis=1)
    o_ref[...] += jnp.dot(w, v_ref[...], preferred_element_type=F32)
    for w_ref, cast_ref in zip(refs[:n_side], refs[n_side + 1:]):
        cast_ref[...] = w_ref[...].astype(cast_ref.dtype)


def _side_cast_specs(w, layer, gi, gj):
    _, r, c = w.shape
    rows = BF16_ROWS
    while rows * gi * gj < r:
        rows *= 2
    assert r % rows == 0
    last = r // rows - 1
    src = pl.BlockSpec((None, rows, c), lambda i, j: (layer, jnp.minimum(i * gj + j, last), 0))
    dst = pl.BlockSpec((rows, c), lambda i, j: (jnp.minimum(i * gj + j, last), 0))
    return src, jax.ShapeDtypeStruct((r, c), BF16), dst


def peer_dense(x, r, u, v, coef, side_casts=(), layer=0):
    t, d = x.shape
    e = u.shape[0]
    tm, te = _row_tile(t, PEER_ROWS), PEER_EXPERTS
    gi, gj = t // tm, e // te
    in_specs = [pl.BlockSpec((tm, d), lambda i, j: (i, 0), pipeline_mode=pl.Buffered(1)),
                pl.BlockSpec((tm, LANES), lambda i, j: (i, 0)),
                pl.BlockSpec((te, d), lambda i, j: (j, 0)),
                pl.BlockSpec((te, d), lambda i, j: (j, 0)),
                pl.BlockSpec((te // N_KEYS, tm, N_KEYS), lambda i, j: (j, i, 0))]
    out_shape = [jax.ShapeDtypeStruct((t, d), F32)]
    out_specs = [pl.BlockSpec((tm, d), lambda i, j: (i, 0), pipeline_mode=pl.Buffered(1))]
    for w in side_casts:
        src, shape, dst = _side_cast_specs(w, layer, gi, gj)
        in_specs.append(src)
        out_shape.append(shape)
        out_specs.append(dst)
    res = pl.pallas_call(
        functools.partial(_peer_kernel, len(side_casts)),
        out_shape=tuple(out_shape),
        grid=(gi, gj),
        in_specs=in_specs,
        out_specs=tuple(out_specs),
        compiler_params=_params(("arbitrary", "arbitrary")),
        name="peer_dense",
    )(x, r, u, v, coef, *side_casts)
    return res if side_casts else res[0]


def kernel(x_prompt, x_sample, w_in, w_out, norm_mix, norm_ffn, lb_logits, hgrn_norm, attn_norm, attn_sink,
           peer_query, peer_sub_keys, peer_u, peer_v, final_norm):
    depth = w_in.shape[0]
    d_model = x_prompt.shape[-1]
    seq_len = x_prompt.shape[1]
    assert x_sample.shape[1] == seq_len, "prompt and sample sequences are stacked and must share a length"
    n_seq = x_prompt.shape[0] + x_sample.shape[0]
    xp = x_prompt.reshape(-1, d_model)
    xs = x_sample.reshape(-1, d_model)
    t_prompt, t_sample = xp.shape[0], xs.shape[0]
    hw = N_HGRN_HEADS * HEAD_DIM
    q_col = 5 * hw
    k_col = q_col + N_ATTN_HEADS * HEAD_DIM
    v_col = k_col + N_KV_HEADS * HEAD_DIM

    cos, sin_signed = rope_tables(seq_len)
    lbl_t = jnp.transpose(lb_logits.astype(F32), (1, 0, 2))

    h = rmsnorm_rows2(xp, xs, norm_mix[0], BF16)
    w_in_bf16 = cast_bf16(w_in, 0)
    x = None
    for l in range(depth):
        p, u_bf16, v_bf16, w_out_bf16 = matmul(h, w_in_bf16, side_casts=(peer_u, peer_v, w_out), layer=l)
        o_h = hgrn2(p, lbl_t, hgrn_norm[l], l, n_seq, seq_len)
        o_a = window_attention(p, cos, sin_signed, attn_sink[l].astype(F32), attn_norm[l], n_seq, seq_len,
                               q_col, k_col, v_col)
        if l == 0:
            x, xg, r = matmul2_residual_norm(o_h, o_a, w_out_bf16, norm_ffn[l], xp, xs)
        else:
            x, xg, r = matmul2_residual_norm(o_h, o_a, w_out_bf16, norm_ffn[l], x)
        ki, kj, gate = peer_query_route(xg, r, cast_bf16(peer_query, l, col_panels=True), peer_sub_keys[l])
        coef = peer_coefficients(ki, kj, gate)
        if l + 1 < depth:
            po, w_in_bf16 = peer_dense(xg, r, u_bf16, v_bf16, coef, side_casts=(w_in,), layer=l + 1)
            x, h = add_rmsnorm_rows(x, po, norm_mix[l + 1], BF16)
        else:
            po = peer_dense(xg, r, u_bf16, v_bf16, coef)
    y_prompt = add_rmsnorm_final(x, po, final_norm, 0, t_prompt)
    y_sample = add_rmsnorm_final(x, po, final_norm, t_prompt, t_sample)
    return (y_prompt.reshape(x_prompt.shape), y_sample.reshape(x_sample.shape))
```
